```python
import math
import jax, jax.numpy as jnp
from jax import lax
import numpy as np

D_MODEL = 4096
BATCH = 4
SEQ = 2048
DEPTH = 2
DEC_BATCH = 32
DEC_SEQ = 1
PAST_LEN = 16384
PAGE_SIZE = 128

HEAD_DIM = 128
A_HEADS = 4
A_DK = 128
A_DV = 256
A_LOWRANK = 16
A_GATE_NORM = 16.0
A_CHUNK = 64
A_Q = A_HEADS * A_DK
A_V = A_HEADS * A_DV
B_HEADS = 12
B_KV_HEADS = 3
B_GROUP = B_HEADS // B_KV_HEADS
B_WINDOW = 128
B_Q = B_HEADS * HEAD_DIM
B_KV = B_KV_HEADS * HEAD_DIM
C_PATTERNS = ((128, 1), (512, 4), (2048, 16))
N_C_GROUPS = 3
C_HEADS = 4
C_W = N_C_GROUPS * C_HEADS * HEAD_DIM
ATTN_BLOCK = 128
ROPE_THETA = 10000.0
IN_SPLITS = (A_Q, A_Q, A_V, A_V, A_LOWRANK, B_Q, B_KV, B_KV, C_W, C_W, C_W)
D_IN = A_Q * 2 + A_V * 2 + A_LOWRANK + B_Q + B_KV * 2 + C_W * 3
D_MIX_OUT = A_V + B_Q + C_HEADS * HEAD_DIM
N_EXPERTS = 32
TOP_K = 4
D_FF = D_MODEL
SWIGLU_LIMIT = 7.0
SWIGLU_ALPHA = 1.702
MOE_BLOCK = 256
DN_ALPHA = (2 * DEPTH) ** 0.25
DN_BETA = (8 * DEPTH) ** -0.25
LN_EPS = 1e-5
RMS_EPS = 1e-6
NEG_INF = -1e30

kernel_name = "hybrid_gla_swa_dilated_moe_step"


def _layernorm(x, g, b):
    xf = x.astype(jnp.float32)
    mu = jnp.mean(xf, axis=-1, keepdims=True)
    var = jnp.mean(jnp.square(xf - mu), axis=-1, keepdims=True)
    y = (xf - mu) * lax.rsqrt(var + LN_EPS) * g.astype(jnp.float32) + b.astype(jnp.float32)
    return y.astype(x.dtype)


def _rope(x, pos):
    half = HEAD_DIM // 2
    inv_freq = ROPE_THETA ** (-jnp.arange(half, dtype=jnp.float32) / half)
    ang = pos.astype(jnp.float32)[:, None] * inv_freq[None, :]
    cos = jnp.cos(ang)[:, None, :]
    sin = jnp.sin(ang)[:, None, :]
    xf = x.astype(jnp.float32)
    x1, x2 = xf[..., :half], xf[..., half:]
    return jnp.concatenate([x1 * cos - x2 * sin, x2 * cos + x1 * sin], axis=-1).astype(x.dtype)


def _masked_softmax(s, mask, sink):
    s = jnp.where(mask, s, NEG_INF)
    m = jnp.max(s, axis=-1)
    if sink is not None:
        m = jnp.maximum(m, sink)
    p = jnp.exp(s - m[..., None])
    den = jnp.sum(p, axis=-1)
    if sink is not None:
        den = den + jnp.exp(sink - m)
    return p, den, m


def _banded_attention(q, k, v, window, sink):
    bsz, length = q.shape[:2]
    blk = min(ATTN_BLOCK, length)
    nb = -(-length // blk)
    pad_end = nb * blk - length
    qp = jnp.pad(q, ((0, 0), (0, pad_end), (0, 0), (0, 0), (0, 0))).reshape(bsz, nb, blk, *q.shape[2:])
    kp = jnp.pad(k, ((0, 0), (window, pad_end), (0, 0), (0, 0)))
    vp = jnp.pad(v, ((0, 0), (window, pad_end), (0, 0), (0, 0)))
    kr = jnp.arange(blk + window)
    idx = jnp.arange(nb)[:, None] * blk + kr[None, :]
    kb = kp[:, idx]
    vb = vp[:, idx]
    s = jnp.einsum('bnqhgd,bnkhd->bnhgqk', qp, kb, preferred_element_type=jnp.float32) * (HEAD_DIM ** -0.5)
    qi = jnp.arange(blk)[:, None]
    kpos = jnp.arange(nb)[:, None, None] * blk - window + kr[None, None, :]
    mask = (kr[None, :] >= qi) & (kr[None, :] <= qi + window) & (kpos >= 0)
    sink_b = None if sink is None else sink.astype(jnp.float32)[..., None]
    p, den, m = _masked_softmax(s, mask[None, :, None, None], sink_b)
    o = jnp.einsum('bnhgqk,bnkhd->bnqhgd', p, vb.astype(jnp.float32)) / jnp.moveaxis(den, -1, 2)[..., None]
    lse = jnp.moveaxis(m + jnp.log(den), -1, 2)
    o = o.reshape(bsz, nb * blk, *q.shape[2:])[:, :length]
    lse = lse.reshape(bsz, nb * blk, *q.shape[2:4])[:, :length]
    return o.astype(q.dtype), lse


def _dilated_prompt(q, k, v, window, dil):
    bsz, tlen, nh, hd = q.shape
    length = tlen // dil

    def split(x):
        return x.reshape(bsz, length, dil, nh, hd).swapaxes(1, 2).reshape(bsz * dil, length, nh, hd)

    o, lse = _banded_attention(split(q)[:, :, :, None], split(k), split(v), window // dil, None)
    o = o[:, :, :, 0].reshape(bsz, dil, length, nh, hd).swapaxes(1, 2).reshape(bsz, tlen, nh, hd)
    lse = lse[..., 0].reshape(bsz, dil, length, nh).swapaxes(1, 2).reshape(bsz, tlen, nh)
    return o, lse


def _dilated_sample(q, k_all, v_all, n_past, window, dil):
    slen = q.shape[1]
    j = jnp.arange(window // dil + 1)
    idx = n_past + jnp.arange(slen)[:, None] - dil * j[None, :]
    valid = idx >= 0
    idx = jnp.maximum(idx, 0)
    kg = k_all[:, idx]
    vg = v_all[:, idx]
    s = jnp.einsum('bshd,bsjhd->bhsj', q, kg, preferred_element_type=jnp.float32) * (HEAD_DIM ** -0.5)
    p, den, m = _masked_softmax(s, valid, None)
    o = jnp.einsum('bhsj,bsjhd->bshd', p, vg.astype(jnp.float32)) / jnp.swapaxes(den, 1, 2)[..., None]
    lse = jnp.swapaxes(m + jnp.log(den), 1, 2)
    return o.astype(q.dtype), lse


def _window_sample(q, k_all, v_all, n_past, window, sink):
    slen = q.shape[1]
    qrel = n_past + jnp.arange(slen)[:, None]
    kr = jnp.arange(k_all.shape[1])[None, :]
    mask = (kr <= qrel) & (kr >= qrel - window)
    s = jnp.einsum('bshgd,bkhd->bhgsk', q, k_all, preferred_element_type=jnp.float32) * (HEAD_DIM ** -0.5)
    p, den, _ = _masked_softmax(s, mask, sink.astype(jnp.float32)[..., None])
    o = jnp.einsum('bhgsk,bkhd->bshgd', p, v_all.astype(jnp.float32)) / jnp.moveaxis(den, -1, 1)[..., None]
    return o.astype(q.dtype)


def _combine_dilated(outs, lses):
    w = jax.nn.softmax(jnp.stack(lses, axis=0), axis=0)
    return jnp.sum(w[..., None] * jnp.stack(outs, axis=0).astype(jnp.float32), axis=0)


def _gla(q, k, v, log_a, s0):
    bsz, tlen, nh, _ = q.shape
    dv = v.shape[-1]
    c = min(A_CHUNK, tlen)
    n = -(-tlen // c)
    pad = n * c - tlen

    def prep(x):
        x = jnp.pad(x.astype(jnp.float32), ((0, 0), (0, pad), (0, 0), (0, 0)))
        return x.reshape(bsz, n, c, nh, x.shape[-1]).transpose(1, 0, 3, 2, 4)

    qc, kc, vc, gc = prep(q * (A_DK ** -0.5)), prep(k), prep(v), prep(log_a)
    causal = jnp.tril(jnp.ones((c, c), dtype=bool))

    def step(state, inp):
        qi, ki, vi, gi = inp
        b = jnp.cumsum(gi, axis=2)
        qg = qi * jnp.exp(b)
        kg = ki * jnp.exp(-b)
        att = jnp.where(causal, jnp.einsum('bhtk,bhsk->bhts', qg, kg), 0.0)
        o = jnp.einsum('bhtk,bhkv->bhtv', qg, state) + jnp.einsum('bhts,bhsv->bhtv', att, vi)
        b_last = b[:, :, -1:]
        state = jnp.exp(b_last[:, :, 0])[..., None] * state + jnp.einsum('bhsk,bhsv->bhkv', ki * jnp.exp(b_last - b), vi)
        return state, o

    s_fin, o = lax.scan(step, s0.astype(jnp.float32), (qc, kc, vc, gc))
    o = o.transpose(1, 0, 3, 2, 4).reshape(bsz, n * c, nh, dv)[:, :tlen]
    return o, s_fin


def _gla_out(o, ga, norm_g):
    bsz, tlen = o.shape[:2]
    o = o * lax.rsqrt(jnp.mean(jnp.square(o), axis=-1, keepdims=True) + RMS_EPS) * norm_g.astype(jnp.float32)
    g = ga.astype(jnp.float32).reshape(bsz, tlen, A_HEADS, A_DV)
    return (o * jax.nn.silu(g)).reshape(bsz, tlen, A_V)


def _split_heads(p, bsz, tlen, pos, w_lr, b_lr):
    points = [int(i) for i in np.cumsum(IN_SPLITS)[:-1]]
    qa, ka, va, ga, lra, qb, kb, vb, qc, kc, vc = jnp.split(p.reshape(bsz, tlen, D_IN), points, axis=-1)
    qa = qa.reshape(bsz, tlen, A_HEADS, A_DK)
    ka = ka.reshape(bsz, tlen, A_HEADS, A_DK)
    va = va.reshape(bsz, tlen, A_HEADS, A_DV)
    log_a = jax.nn.log_sigmoid((jnp.dot(lra, w_lr) + b_lr).astype(jnp.float32)) / A_GATE_NORM
    log_a = log_a.reshape(bsz, tlen, A_HEADS, A_DK)
    qb = _rope(qb.reshape(bsz, tlen, B_HEADS, HEAD_DIM), pos).reshape(bsz, tlen, B_KV_HEADS, B_GROUP, HEAD_DIM)
    kb = _rope(kb.reshape(bsz, tlen, B_KV_HEADS, HEAD_DIM), pos)
    vb = vb.reshape(bsz, tlen, B_KV_HEADS, HEAD_DIM)
    qc = _rope(qc.reshape(bsz, tlen, N_C_GROUPS * C_HEADS, HEAD_DIM), pos).reshape(bsz, tlen, N_C_GROUPS, C_HEADS, HEAD_DIM)
    kc = _rope(kc.reshape(bsz, tlen, N_C_GROUPS * C_HEADS, HEAD_DIM), pos).reshape(bsz, tlen, N_C_GROUPS, C_HEADS, HEAD_DIM)
    vc = vc.reshape(bsz, tlen, N_C_GROUPS, C_HEADS, HEAD_DIM)
    return qa, ka, va, log_a, ga, qb, kb, vb, qc, kc, vc


def _mix_prompt(heads, sink, norm_g):
    qa, ka, va, log_a, ga, qb, kb, vb, qc, kc, vc = heads
    bsz, tlen = qa.shape[:2]
    s0 = jnp.zeros((bsz, A_HEADS, A_DK, A_DV), jnp.float32)
    oa, sa = _gla(qa, ka, va, log_a, s0)
    ya = _gla_out(oa, ga, norm_g)
    ob, _ = _banded_attention(qb, kb, vb, B_WINDOW, sink)
    lb = min(B_WINDOW, tlen)
    kv_b = jnp.stack([kb[:, tlen - lb:], vb[:, tlen - lb:]], axis=1)
    outs, lses, kv_c = [], [], []
    for g, (win, dil) in enumerate(C_PATTERNS):
        o, lse = _dilated_prompt(qc[:, :, g], kc[:, :, g], vc[:, :, g], win, dil)
        outs.append(o)
        lses.append(lse)
        lc = min(win, tlen)
        kv_c.append(jnp.stack([kc[:, tlen - lc:, g], vc[:, tlen - lc:, g]], axis=1))
    yc = _combine_dilated(outs, lses)
    mix = jnp.concatenate([ya, ob.reshape(bsz, tlen, B_Q).astype(jnp.float32), yc.reshape(bsz, tlen, C_HEADS * HEAD_DIM)], axis=-1)
    return mix, (kv_b, kv_c[0], kv_c[1], kv_c[2], sa)


def _mix_sample(heads, kv_b_cache, kv_c_caches, s0, sink, norm_g):
    qa, ka, va, log_a, ga, qb, kb, vb, qc, kc, vc = heads
    bsz, slen = qa.shape[:2]
    oa, sa = _gla(qa, ka, va, log_a, s0.astype(jnp.float32))
    ya = _gla_out(oa, ga, norm_g)
    lb = kv_b_cache.shape[2]
    kb_all = jnp.concatenate([kv_b_cache[:, 0], kb], axis=1)
    vb_all = jnp.concatenate([kv_b_cache[:, 1], vb], axis=1)
    ob = _window_sample(qb, kb_all, vb_all, lb, B_WINDOW, sink)
    new_b = jnp.stack([kb, vb], axis=1)
    outs, lses, new_c = [], [], []
    for g, (win, dil) in enumerate(C_PATTERNS):
        cache = kv_c_caches[g]
        lc = cache.shape[2]
        k_all = jnp.concatenate([cache[:, 0], kc[:, :, g]], axis=1)
        v_all = jnp.concatenate([cache[:, 1], vc[:, :, g]], axis=1)
        o, lse = _dilated_sample(qc[:, :, g], k_all, v_all, lc, win, dil)
        outs.append(o)
        lses.append(lse)
        new_c.append(jnp.stack([kc[:, :, g], vc[:, :, g]], axis=1))
    yc = _combine_dilated(outs, lses)
    mix = jnp.concatenate([ya, ob.reshape(bsz, slen, B_Q).astype(jnp.float32), yc.reshape(bsz, slen, C_HEADS * HEAD_DIM)], axis=-1)
    return mix, (new_b, new_c[0], new_c[1], new_c[2], sa)


def _moe(h, l, router_w, router_b, w_gate, b_gate, w_up, b_up, w_down, b_down):
    n_tok = h.shape[0]
    logits = jnp.dot(h, router_w[l], preferred_element_type=jnp.float32) + router_b[l].astype(jnp.float32)
    top_v, top_e = lax.top_k(logits, TOP_K)
    gates = jax.nn.softmax(top_v, axis=-1)
    m = n_tok * TOP_K
    e_flat = top_e.reshape(m)
    tok = jnp.repeat(jnp.arange(n_tok), TOP_K)
    blk = max(8, min(MOE_BLOCK, m // N_EXPERTS))
    n_blk = -(-m // blk) + N_EXPERTS
    onehot = jax.nn.one_hot(e_flat, N_EXPERTS, dtype=jnp.int32)
    counts = jnp.sum(onehot, axis=0)
    rank = jnp.take_along_axis(jnp.cumsum(onehot, axis=0) - onehot, e_flat[:, None], axis=1)[:, 0]
    blocks_per = (counts + blk - 1) // blk
    blocks_end = jnp.cumsum(blocks_per)
    dest = (blocks_end - blocks_per)[e_flat] * blk + rank
    block_expert = jnp.minimum(jnp.searchsorted(blocks_end, jnp.arange(n_blk), side='right'), N_EXPERTS - 1)
    xb = jnp.zeros((n_blk * blk, D_MODEL), h.dtype).at[dest].set(h[tok]).reshape(n_blk, blk, D_MODEL)

    def expert_block(args):
        xi, e = args
        g = jnp.dot(xi, w_gate[l, e], preferred_element_type=jnp.float32) + b_gate[l, e].astype(jnp.float32)
        u = jnp.dot(xi, w_up[l, e], preferred_element_type=jnp.float32) + b_up[l, e].astype(jnp.float32)
        g = jnp.minimum(g, SWIGLU_LIMIT)
        u = jnp.clip(u, -SWIGLU_LIMIT, SWIGLU_LIMIT)
        a = g * jax.nn.sigmoid(SWIGLU_ALPHA * g) * (u + 1.0)
        y = jnp.dot(a.astype(xi.dtype), w_down[l, e], preferred_element_type=jnp.float32) + b_down[l, e].astype(jnp.float32)
        return y.astype(xi.dtype)

    yb = lax.map(expert_block, (xb, block_expert)).reshape(n_blk * blk, D_MODEL)
    y = yb[dest].reshape(n_tok, TOP_K, D_MODEL).astype(jnp.float32) * gates[..., None]
    return jnp.sum(y, axis=1).astype(h.dtype)


def setup_inputs(seed: int = 0) -> dict:
    key = jax.random.key(seed)
    ks = jax.random.split(key, 29)
    f32 = jnp.float32

    def u(k, shape, std):
        bound = std * math.sqrt(3.0)
        return jax.random.uniform(k, shape, f32, -bound, bound)

    def nrm(k, shape, std=1.0):
        return std * jax.random.normal(k, shape, f32)

    lb = min(B_WINDOW, PAST_LEN)
    lc = [min(w, PAST_LEN) for w, _ in C_PATTERNS]
    return {
        'x_prompt': nrm(ks[0], (BATCH, SEQ, D_MODEL)),
        'x_sample': nrm(ks[1], (DEC_BATCH, DEC_SEQ, D_MODEL)),
        'cache_b_kv': nrm(ks[2], (DEC_BATCH, DEPTH, 2, lb, B_KV_HEADS, HEAD_DIM)),
        'cache_c1_kv': nrm(ks[3], (DEC_BATCH, DEPTH, 2, lc[0], C_HEADS, HEAD_DIM)),
        'cache_c2_kv': nrm(ks[4], (DEC_BATCH, DEPTH, 2, lc[1], C_HEADS, HEAD_DIM)),
        'cache_c3_kv': nrm(ks[5], (DEC_BATCH, DEPTH, 2, lc[2], C_HEADS, HEAD_DIM)),
        'state_gla': nrm(ks[6], (DEC_BATCH, DEPTH, A_HEADS, A_DK, A_DV)),
        'ln_in_g': 1.0 + nrm(ks[7], (D_MODEL,), 0.02),
        'ln_in_b': nrm(ks[8], (D_MODEL,), 0.02),
        'w_in': u(ks[9], (DEPTH, D_MODEL, D_IN), D_MODEL ** -0.5),
        'b_in': nrm(ks[10], (DEPTH, D_IN), 0.02),
        'gla_w_gate': u(ks[11], (DEPTH, A_LOWRANK, A_Q), A_LOWRANK ** -0.5),
        'gla_b_gate': nrm(ks[12], (DEPTH, A_Q), 0.1),
        'gla_norm_g': 1.0 + nrm(ks[13], (DEPTH, A_DV), 0.02),
        'attn_sinks': nrm(ks[14], (DEPTH, B_KV_HEADS, B_GROUP), 1.0),
        'w_out': u(ks[15], (DEPTH, D_MIX_OUT, D_MODEL), DN_BETA * D_MIX_OUT ** -0.5),
        'b_out': nrm(ks[16], (DEPTH, D_MODEL), 0.02),
        'ln1_g': 1.0 + nrm(ks[17], (DEPTH, D_MODEL), 0.02),
        'ln1_b': nrm(ks[18], (DEPTH, D_MODEL), 0.02),
        'router_w': u(ks[19], (DEPTH, D_MODEL, N_EXPERTS), D_MODEL ** -0.5),
        'router_b': nrm(ks[20], (DEPTH, N_EXPERTS), 0.01),
        'w_gate': u(ks[21], (DEPTH, N_EXPERTS, D_MODEL, D_FF), D_MODEL ** -0.5),
        'b_gate': nrm(ks[22], (DEPTH, N_EXPERTS, D_FF), 0.02),
        'w_up': u(ks[23], (DEPTH, N_EXPERTS, D_MODEL, D_FF), D_MODEL ** -0.5),
        'b_up': nrm(ks[24], (DEPTH, N_EXPERTS, D_FF), 0.02),
        'w_down': u(ks[25], (DEPTH, N_EXPERTS, D_FF, D_MODEL), DN_BETA * D_FF ** -0.5),
        'b_down': nrm(ks[26], (DEPTH, N_EXPERTS, D_MODEL), 0.02),
        'ln2_g': 1.0 + nrm(ks[27], (DEPTH, D_MODEL), 0.02),
        'ln2_b': nrm(ks[28], (DEPTH, D_MODEL), 0.02),
    }


def reference(x_prompt, x_sample, cache_b_kv, cache_c1_kv, cache_c2_kv, cache_c3_kv, state_gla,
              ln_in_g, ln_in_b, w_in, b_in, gla_w_gate, gla_b_gate, gla_norm_g, attn_sinks,
              w_out, b_out, ln1_g, ln1_b, router_w, router_b, w_gate, b_gate, w_up, b_up,
              w_down, b_down, ln2_g, ln2_b):
    bp, tlen, _ = x_prompt.shape
    bs, slen, _ = x_sample.shape
    n_p, n_s = bp * tlen, bs * slen
    pos_p = jnp.arange(tlen, dtype=jnp.int32)
    pos_s = PAST_LEN + jnp.arange(slen, dtype=jnp.int32)
    c_caches = (cache_c1_kv, cache_c2_kv, cache_c3_kv)
    h = jnp.concatenate([_layernorm(x_prompt, ln_in_g, ln_in_b).reshape(n_p, D_MODEL),
                         _layernorm(x_sample, ln_in_g, ln_in_b).reshape(n_s, D_MODEL)], axis=0)
    new_p = [[] for _ in range(2 + N_C_GROUPS)]
    new_s = [[] for _ in range(2 + N_C_GROUPS)]
    for l in range(DEPTH):
        proj = jnp.dot(h, w_in[l]) + b_in[l]
        heads_p = _split_heads(proj[:n_p], bp, tlen, pos_p, gla_w_gate[l], gla_b_gate[l])
        heads_s = _split_heads(proj[n_p:], bs, slen, pos_s, gla_w_gate[l], gla_b_gate[l])
        mix_p, st_p = _mix_prompt(heads_p, attn_sinks[l], gla_norm_g[l])
        mix_s, st_s = _mix_sample(heads_s, cache_b_kv[:, l], (c_caches[0][:, l], c_caches[1][:, l], c_caches[2][:, l]),
                                  state_gla[:, l], attn_sinks[l], gla_norm_g[l])
        for i in range(2 + N_C_GROUPS):
            new_p[i].append(st_p[i])
            new_s[i].append(st_s[i])
        mix = jnp.concatenate([mix_p.reshape(n_p, D_MIX_OUT), mix_s.reshape(n_s, D_MIX_OUT)], axis=0).astype(h.dtype)
        h = _layernorm(DN_ALPHA * h + jnp.dot(mix, w_out[l]) + b_out[l], ln1_g[l], ln1_b[l])
        h = _layernorm(DN_ALPHA * h + _moe(h, l, router_w, router_b, w_gate, b_gate, w_up, b_up, w_down, b_down),
                       ln2_g[l], ln2_b[l])
    y_prompt = h[:n_p].reshape(bp, tlen, D_MODEL)
    y_sample = h[n_p:].reshape(bs, slen, D_MODEL)
    return (y_prompt, y_sample,
            jnp.stack(new_p[0], axis=1), jnp.stack(new_s[0], axis=1),
            jnp.stack(new_p[1], axis=1), jnp.stack(new_s[1], axis=1),
            jnp.stack(new_p[2], axis=1), jnp.stack(new_s[2], axis=1),
            jnp.stack(new_p[3], axis=1), jnp.stack(new_s[3], axis=1),
            jnp.stack(new_p[4], axis=1), jnp.stack(new_s[4], axis=1))
```

```python
import functools

import numpy as np
import jax
import jax.numpy as jnp
from jax import lax
from jax.experimental import pallas as pl
from jax.experimental.pallas import tpu as pltpu

F32 = jnp.float32
BF16 = jnp.bfloat16
HIGHEST = lax.Precision.HIGHEST

LANES = 128
SUBLANES = 8

D_MODEL = 4096
DEPTH = 2
PAST_LEN = 16384
HEAD_DIM = 128
A_HEADS, A_DK, A_DV, A_LOWRANK = 4, 128, 256, 16
A_GATE_NORM = 16.0
A_CHUNK = 64
B_HEADS, B_KV_HEADS = 12, 3
B_GROUP = B_HEADS // B_KV_HEADS
B_WINDOW = 128
C_PATTERNS = ((128, 1), (512, 4), (2048, 16))
N_C_GROUPS = len(C_PATTERNS)
C_HEADS = 4
ATTN_BLOCK = 128
ROPE_THETA = 10000.0
N_EXPERTS = 32
TOP_K = 4
SWIGLU_LIMIT = 7.0
SWIGLU_ALPHA = 1.702
DN_ALPHA = (2 * DEPTH) ** 0.25
LN_EPS = 1e-5
RMS_EPS = 1e-6
NEG_INF = -1e30
ATTN_SCALE = HEAD_DIM ** -0.5

_O_QA, _O_KA, _O_VA, _O_GA, _O_LR, _O_QB, _O_KB, _O_VB, _O_QC, _O_KC, _O_VC, _O_END = (
    0, 512, 1024, 2048, 3072, 3088, 4624, 5008, 5392, 6928, 8464, 10000)
R_QB, R_KB, R_QC, R_KC = 0, 12, 15, 27
P_QA, P_KA, P_VA, P_GA, P_VB, P_VC, P_LR = 0, 4, 8, 16, 24, 27, 39
U_PROJ = 40
W_PROJ = U_PROJ * LANES

S_ROWS = 16 + SUBLANES * N_C_GROUPS

MOE_TM = 256
MOE_TN_UP = 512
MOE_TN_DOWN = 512


def _cparams(sem, vmem_mb):
    return pltpu.CompilerParams(dimension_semantics=sem, vmem_limit_bytes=vmem_mb * 1024 * 1024)


def _ln_rows(x, g, b):
    mu = jnp.mean(x, axis=-1, keepdims=True)
    xc = x - mu
    var = jnp.mean(xc * xc, axis=-1, keepdims=True)
    return xc * lax.rsqrt(var + LN_EPS) * g + b


def _ln_in_body(x_ref, g_ref, b_ref, o_ref, ob_ref):
    y = _ln_rows(x_ref[...], g_ref[...], b_ref[...])
    o_ref[...] = y
    ob_ref[...] = y.astype(BF16)


def _ln_in(x, g, b, tm):
    m, d = x.shape
    row = pl.BlockSpec((tm, d), lambda i: (i, 0))
    vec = pl.BlockSpec((1, d), lambda i: (0, 0))
    return pl.pallas_call(
        _ln_in_body, grid=(m // tm,), in_specs=[row, vec, vec], out_specs=[row, row],
        out_shape=[jax.ShapeDtypeStruct((m, d), F32), jax.ShapeDtypeStruct((m, d), BF16)],
        compiler_params=_cparams(("parallel",), 48), name="ln_in",
    )(x, g.reshape(1, d), b.reshape(1, d))


def _proj_body(*refs, rope, tn):
    if rope:
        x_ref, w_ref, b_ref, cos_ref, sin_ref, o_ref = refs
    else:
        x_ref, w_ref, b_ref, o_ref = refs
    acc = jnp.dot(x_ref[...], w_ref[...], preferred_element_type=F32) + b_ref[...]
    if not rope:
        o_ref[...] = acc
        return
    cos = cos_ref[...]
    sin = sin_ref[...]
    for u in range(tn // LANES):
        seg = acc[:, u * LANES:(u + 1) * LANES]
        o_ref[:, u * LANES:(u + 1) * LANES] = seg * cos + pltpu.roll(seg, HEAD_DIM // 2, 1) * sin


def _proj(xb, w, b, cos=None, sin=None, *, tm, tn=1024):
    m, k = xb.shape
    n = w.shape[1]
    rope = cos is not None
    in_specs = [pl.BlockSpec((tm, k), lambda j, i: (i, 0)),
                pl.BlockSpec((k, tn), lambda j, i: (0, j)),
                pl.BlockSpec((1, tn), lambda j, i: (0, j))]
    args = [xb, w, b]
    if rope:
        tab = pl.BlockSpec((tm, LANES), lambda j, i: (i, 0))
        in_specs += [tab, tab]
        args += [cos, sin]
    return pl.pallas_call(
        functools.partial(_proj_body, rope=rope, tn=tn), grid=(n // tn, m // tm),
        in_specs=in_specs, out_specs=pl.BlockSpec((tm, tn), lambda j, i: (i, j)),
        out_shape=jax.ShapeDtypeStruct((m, n), F32),
        compiler_params=_cparams(("parallel", "parallel"), 48),
        name="proj_rope" if rope else "proj_plain",
    )(*args)


def _log_sigmoid(z):
    return jnp.minimum(z, 0.0) - jnp.log(1.0 + jnp.exp(-jnp.abs(z)))


def _rms_gate(o, ng, ga):
    o = o * lax.rsqrt(jnp.mean(o * o, axis=-1, keepdims=True) + RMS_EPS) * ng
    return o * (ga * jax.nn.sigmoid(ga))


def _gla_body(q_ref, k_ref, v_ref, ga_ref, lr_ref, wlr_ref, blr_ref, ng_ref, ya_ref, st_ref, s_scr, *, n_sub):
    c = pl.program_id(2)

    @pl.when(c == 0)
    def _():
        s_scr[...] = jnp.zeros_like(s_scr)

    ch = A_CHUNK
    rr = lax.broadcasted_iota(jnp.int32, (ch, ch), 0)
    cc = lax.broadcasted_iota(jnp.int32, (ch, ch), 1)
    causal = rr >= cc
    tri = causal.astype(F32)
    ones = jnp.ones((ch, A_DK), F32)
    for u in range(n_sub):
        sl = pl.ds(u * ch, ch)
        z = jnp.dot(lr_ref[sl, :], wlr_ref[...], precision=HIGHEST, preferred_element_type=F32) + blr_ref[...]
        g = _log_sigmoid(z) / A_GATE_NORM
        b = jnp.dot(tri, g, precision=HIGHEST, preferred_element_type=F32)
        b_last = b[ch - 1:ch, :]
        b_last_col = lax.dot_general(g, ones, (((0,), (0,)), ((), ())), precision=HIGHEST,
                                     preferred_element_type=F32)
        q = q_ref[sl, :] * (A_DK ** -0.5)
        k = k_ref[sl, :]
        v = v_ref[sl, :].astype(BF16)
        qg = (q * jnp.exp(b)).astype(BF16)
        kg = (k * jnp.exp(-b)).astype(BF16)
        kd = (k * jnp.exp(b_last - b)).astype(BF16)
        att = lax.dot_general(qg, kg, (((1,), (1,)), ((), ())), preferred_element_type=F32)
        att = jnp.where(causal, att, 0.0).astype(BF16)
        s = s_scr[...]
        o = (jnp.dot(qg, s.astype(BF16), preferred_element_type=F32)
             + jnp.dot(att, v, preferred_element_type=F32))
        decay = jnp.exp(b_last_col)
        decay = jnp.concatenate([decay, decay], axis=1)
        s_scr[...] = decay * s + lax.dot_general(kd, v, (((0,), (0,)), ((), ())), preferred_element_type=F32)
        ya_ref[sl, :] = _rms_gate(o, ng_ref[...], ga_ref[sl, :])

    @pl.when(c == pl.num_programs(2) - 1)
    def _():
        st_ref[...] = s_scr[...]


def _gla_prompt(p_arr, wlr_pad, blr, ng, bsz, tlen, rb=256):
    nblk = tlen // rb
    n_sub = rb // A_CHUNK

    def rows(b, c):
        return b * nblk + c

    in_specs = [
        pl.BlockSpec((rb, A_DK), lambda b, h, c: (rows(b, c), P_QA + h)),
        pl.BlockSpec((rb, A_DK), lambda b, h, c: (rows(b, c), P_KA + h)),
        pl.BlockSpec((rb, A_DV), lambda b, h, c: (rows(b, c), P_VA // 2 + h)),
        pl.BlockSpec((rb, A_DV), lambda b, h, c: (rows(b, c), P_GA // 2 + h)),
        pl.BlockSpec((rb, LANES), lambda b, h, c: (rows(b, c), P_LR)),
        pl.BlockSpec((LANES, A_DK), lambda b, h, c: (0, h)),
        pl.BlockSpec((1, A_DK), lambda b, h, c: (0, h)),
        pl.BlockSpec((1, A_DV), lambda b, h, c: (0, 0)),
    ]
    out_specs = [
        pl.BlockSpec((rb, A_DV), lambda b, h, c: (rows(b, c), h)),
        pl.BlockSpec((None, None, A_DK, A_DV), lambda b, h, c: (b, h, 0, 0)),
    ]
    return pl.pallas_call(
        functools.partial(_gla_body, n_sub=n_sub), grid=(bsz, A_HEADS, nblk),
        in_specs=in_specs, out_specs=out_specs,
        out_shape=[jax.ShapeDtypeStruct((bsz * tlen, A_HEADS * A_DV), F32),
                   jax.ShapeDtypeStruct((bsz, A_HEADS, A_DK, A_DV), F32)],
        scratch_shapes=[pltpu.VMEM((A_DK, A_DV), F32)],
        compiler_params=_cparams(("parallel", "parallel", "arbitrary"), 32), name="gla_prompt",
    )(p_arr, p_arr, p_arr, p_arr, p_arr, wlr_pad, blr, ng)


def _band_body(*refs, group, use_sink, has_prev, want_lse):
    refs = list(refs)
    sink_ref = refs.pop(0) if use_sink else None
    q_ref = refs.pop(0)
    kc_ref = refs.pop(0)
    vc_ref = refs.pop(0)
    kp_ref = refs.pop(0) if has_prev else None
    vp_ref = refs.pop(0) if has_prev else None
    o_ref = refs.pop(0)
    lse_ref = refs.pop(0) if want_lse else None

    blk = ATTN_BLOCK
    hk = pl.program_id(1)
    n = pl.program_id(2)
    q = q_ref[...]
    if group > 1:
        q = jnp.concatenate([q[:, g * HEAD_DIM:(g + 1) * HEAD_DIM] for g in range(group)], axis=0)
    qb = q.astype(BF16)
    rows = group * blk
    r = lax.broadcasted_iota(jnp.int32, (rows, blk), 0)
    if group > 1:
        r = r % blk
    c = lax.broadcasted_iota(jnp.int32, (rows, blk), 1)
    nt = (((1,), (1,)), ((), ()))

    s_c = lax.dot_general(qb, kc_ref[...].astype(BF16), nt, preferred_element_type=F32) * ATTN_SCALE
    s_c = jnp.where(c <= r, s_c, NEG_INF)
    m = jnp.max(s_c, axis=1, keepdims=True)
    if has_prev:
        s_p = lax.dot_general(qb, kp_ref[...].astype(BF16), nt, preferred_element_type=F32) * ATTN_SCALE
        s_p = jnp.where((c >= r) & (n > 0), s_p, NEG_INF)
        m = jnp.maximum(m, jnp.max(s_p, axis=1, keepdims=True))
    if use_sink:
        sink = jnp.concatenate([jnp.full((blk, 1), sink_ref[hk * group + g], F32) for g in range(group)], axis=0)
        m = jnp.maximum(m, sink)
    p_c = jnp.exp(s_c - m)
    den = jnp.sum(p_c, axis=1, keepdims=True)
    acc = jnp.dot(p_c.astype(BF16), vc_ref[...].astype(BF16), preferred_element_type=F32)
    if has_prev:
        p_p = jnp.exp(s_p - m)
        den = den + jnp.sum(p_p, axis=1, keepdims=True)
        acc = acc + jnp.dot(p_p.astype(BF16), vp_ref[...].astype(BF16), preferred_element_type=F32)
    if use_sink:
        den = den + jnp.exp(sink - m)
    o = acc / den
    if group > 1:
        o = jnp.concatenate([o[g * blk:(g + 1) * blk, :] for g in range(group)], axis=1)
    o_ref[...] = o
    if want_lse:
        lse_ref[...] = jnp.broadcast_to(m + jnp.log(den), (blk, HEAD_DIM))


def _band_attention(r_arr, p_arr, sink, *, bsz, tlen, dil, heads, group, uq, uk, uv, want_lse):
    blk = ATTN_BLOCK
    length = tlen // dil
    nblk = length // blk
    has_prev = nblk > 1
    use_sink = sink is not None
    rv = r_arr.reshape(bsz * length, dil * W_PROJ)
    pv = p_arr.reshape(bsz * length, dil * W_PROJ)
    wo = heads * group * HEAD_DIM

    def row(s, n):
        return (s // dil) * nblk + n

    def prow(s, n):
        return (s // dil) * nblk + jnp.maximum(n - 1, 0)

    q_spec = pl.BlockSpec((blk, group * HEAD_DIM), lambda s, h, n: (row(s, n), ((s % dil) * U_PROJ + uq) // group + h))
    kc_spec = pl.BlockSpec((blk, HEAD_DIM), lambda s, h, n: (row(s, n), (s % dil) * U_PROJ + uk + h))
    vc_spec = pl.BlockSpec((blk, HEAD_DIM), lambda s, h, n: (row(s, n), (s % dil) * U_PROJ + uv + h))
    in_specs, args = [], []
    if use_sink:
        in_specs.append(pl.BlockSpec(memory_space=pltpu.SMEM))
        args.append(sink)
    in_specs += [q_spec, kc_spec, vc_spec]
    args += [rv, rv, pv]
    if has_prev:
        in_specs += [pl.BlockSpec((blk, HEAD_DIM), lambda s, h, n: (prow(s, n), (s % dil) * U_PROJ + uk + h)),
                     pl.BlockSpec((blk, HEAD_DIM), lambda s, h, n: (prow(s, n), (s % dil) * U_PROJ + uv + h))]
        args += [rv, pv]
    out_specs = [pl.BlockSpec((blk, group * HEAD_DIM), lambda s, h, n: (row(s, n), (s % dil) * heads + h))]
    out_shape = [jax.ShapeDtypeStruct((bsz * length, dil * wo), F32)]
    if want_lse:
        out_specs.append(pl.BlockSpec((blk, HEAD_DIM), lambda s, h, n: (row(s, n), (s % dil) * heads + h)))
        out_shape.append(jax.ShapeDtypeStruct((bsz * length, dil * wo), F32))
    outs = pl.pallas_call(
        functools.partial(_band_body, group=group, use_sink=use_sink, has_prev=has_prev, want_lse=want_lse),
        grid=(bsz * dil, heads, nblk), in_specs=in_specs, out_specs=out_specs, out_shape=out_shape,
        compiler_params=_cparams(("parallel", "parallel", "parallel"), 32), name=f"band_attn_d{dil}",
    )(*args)
    return [o.reshape(bsz * tlen, wo) for o in outs]


def _sample_body(sink_ref, q_ref, kn_ref, vn_ref, qk_ref, va_ref, ga_ref, lr_ref, st_ref,
                 cb_ref, c1_ref, c2_ref, c3_ref, wlr_ref, blr_ref, ng_ref,
                 ya_ref, ob_ref, yc_ref, sn_ref):
    nt = (((1,), (1,)), ((), ()))

    z = jnp.dot(lr_ref[...], wlr_ref[...], precision=HIGHEST, preferred_element_type=F32)[0:1, :] + blr_ref[...]
    a_all = jnp.exp(_log_sigmoid(z) / A_GATE_NORM)
    qk = qk_ref[...]
    rid16 = lax.broadcasted_iota(jnp.int32, (16, 1), 0)
    rows = jnp.zeros((16, A_DK), F32)
    for h in range(A_HEADS):
        a_h = jnp.broadcast_to(a_all[:, h * A_DK:(h + 1) * A_DK], (16, A_DK))
        k_h = jnp.broadcast_to(qk[A_HEADS + h:A_HEADS + h + 1, :], (16, A_DK))
        q_h = jnp.broadcast_to(qk[h:h + 1, :], (16, A_DK)) * (A_DK ** -0.5)
        rows = jnp.where(rid16 == 3 * h, a_h, rows)
        rows = jnp.where(rid16 == 3 * h + 1, k_h, rows)
        rows = jnp.where(rid16 == 3 * h + 2, q_h, rows)
    cols = jnp.concatenate([rows, jnp.zeros((LANES - 16, A_DK), F32)], axis=0).T
    for h in range(A_HEADS):
        a_col, k_col, q_col = cols[:, 3 * h:3 * h + 1], cols[:, 3 * h + 1:3 * h + 2], cols[:, 3 * h + 2:3 * h + 3]
        s_new = a_col * st_ref[h] + k_col * va_ref[h:h + 1, :]
        sn_ref[h] = s_new
        o = jnp.sum(q_col * s_new, axis=0, keepdims=True)
        ya_ref[h:h + 1, :] = _rms_gate(o, ng_ref[...], ga_ref[h:h + 1, :])

    q = q_ref[...]
    q16 = q.astype(BF16)
    rid = lax.broadcasted_iota(jnp.int32, (S_ROWS, 1), 0)

    def pick_b(j):
        return (rid < B_HEADS) & (rid // B_GROUP == j)

    def pick_c(g, hh):
        return rid == 16 + SUBLANES * g + hh

    c_refs = (c1_ref, c2_ref, c3_ref)
    s = jnp.zeros((S_ROWS, ATTN_BLOCK), F32)
    for j in range(B_KV_HEADS):
        kj = cb_ref[0, :, j * HEAD_DIM:(j + 1) * HEAD_DIM].astype(BF16)
        s = jnp.where(pick_b(j), lax.dot_general(q16, kj, nt, preferred_element_type=F32), s)
    for g in range(N_C_GROUPS):
        for hh in range(C_HEADS):
            kh = c_refs[g][0, :, hh * HEAD_DIM:(hh + 1) * HEAD_DIM].astype(BF16)
            s = jnp.where(pick_c(g, hh), lax.dot_general(q16, kh, nt, preferred_element_type=F32), s)
    s = s * ATTN_SCALE
    s_n = jnp.sum(q * kn_ref[...], axis=1, keepdims=True) * ATTN_SCALE
    sink = jnp.full((S_ROWS, 1), NEG_INF, F32)
    for i in range(B_HEADS):
        sink = jnp.where(rid == i, sink_ref[i], sink)
    m = jnp.maximum(jnp.maximum(jnp.max(s, axis=1, keepdims=True), s_n), sink)
    p = jnp.exp(s - m)
    p_n = jnp.exp(s_n - m)
    den = jnp.sum(p, axis=1, keepdims=True) + p_n + jnp.exp(sink - m)
    p16 = p.astype(BF16)
    acc = p_n * vn_ref[...]
    for j in range(B_KV_HEADS):
        vj = cb_ref[1, :, j * HEAD_DIM:(j + 1) * HEAD_DIM].astype(BF16)
        acc = acc + jnp.where(pick_b(j), jnp.dot(p16, vj, preferred_element_type=F32), 0.0)
    for g in range(N_C_GROUPS):
        for hh in range(C_HEADS):
            vh = c_refs[g][1, :, hh * HEAD_DIM:(hh + 1) * HEAD_DIM].astype(BF16)
            acc = acc + jnp.where(pick_c(g, hh), jnp.dot(p16, vh, preferred_element_type=F32), 0.0)
    o = acc / den
    lse = m + jnp.log(den)
    ob_ref[...] = o[0:16, :]
    ls = [lse[16 + SUBLANES * g:24 + SUBLANES * g, :] for g in range(N_C_GROUPS)]
    os_ = [o[16 + SUBLANES * g:24 + SUBLANES * g, :] for g in range(N_C_GROUPS)]
    lmax = jnp.maximum(jnp.maximum(ls[0], ls[1]), ls[2])
    es = [jnp.exp(l - lmax) for l in ls]
    yc_ref[...] = (es[0] * os_[0] + es[1] * os_[1] + es[2] * os_[2]) / (es[0] + es[1] + es[2])


def _sample_mixers(layer, r_s, p_s, state_gla, cache_b, caches_c, sink, wlr_pad, blr, ng):
    bs = r_s.shape[0]
    r3 = r_s.reshape(bs, U_PROJ, LANES)
    p3 = p_s.reshape(bs, U_PROJ, LANES)
    z4 = jnp.zeros((bs, 4, LANES), F32)

    def qrows(x3, ub, uc, rep):
        parts = [jnp.repeat(x3[:, ub:ub + B_KV_HEADS], B_GROUP, axis=1) if rep else x3[:, ub:ub + B_HEADS], z4]
        for g in range(N_C_GROUPS):
            parts += [x3[:, uc + C_HEADS * g:uc + C_HEADS * (g + 1)], z4]
        return jnp.concatenate(parts, axis=1)

    q_rows = qrows(r3, R_QB, R_QC, False)
    kn_rows = qrows(r3, R_KB, R_KC, True)
    vn_rows = qrows(p3, P_VB, P_VC, True)
    qk_a = p3[:, P_QA:P_VA]
    v_a = p_s[:, P_VA * LANES:P_GA * LANES].reshape(bs, A_HEADS, A_DV)
    g_a = p_s[:, P_GA * LANES:P_VB * LANES].reshape(bs, A_HEADS, A_DV)
    lr = jnp.broadcast_to(p3[:, P_LR:P_LR + 1], (bs, SUBLANES, LANES))

    cb = cache_b.reshape(bs, DEPTH, 2, B_WINDOW, B_KV_HEADS * HEAD_DIM)
    cc = [c.reshape(bs, DEPTH, 2, ATTN_BLOCK, dil * C_HEADS * HEAD_DIM) for c, (_, dil) in zip(caches_c, C_PATTERNS)]

    def tok(rows, width):
        return pl.BlockSpec((None, rows, width), lambda b: (b, 0, 0))

    in_specs = [
        pl.BlockSpec(memory_space=pltpu.SMEM),
        tok(S_ROWS, LANES), tok(S_ROWS, LANES), tok(S_ROWS, LANES),
        tok(2 * A_HEADS, LANES), tok(A_HEADS, A_DV), tok(A_HEADS, A_DV), tok(SUBLANES, LANES),
        pl.BlockSpec((None, None, A_HEADS, A_DK, A_DV), lambda b: (b, layer, 0, 0, 0)),
        pl.BlockSpec((None, None, 2, B_WINDOW, B_KV_HEADS * HEAD_DIM), lambda b: (b, layer, 0, 0, 0)),
    ] + [pl.BlockSpec((None, None, 2, ATTN_BLOCK, C_HEADS * HEAD_DIM), lambda b: (b, layer, 0, 0, 0)) for _ in cc] + [
        pl.BlockSpec((LANES, A_HEADS * A_DK), lambda b: (0, 0)),
        pl.BlockSpec((1, A_HEADS * A_DK), lambda b: (0, 0)),
        pl.BlockSpec((1, A_DV), lambda b: (0, 0)),
    ]
    out_specs = [
        pl.BlockSpec((None, A_HEADS, A_DV), lambda b: (b, 0, 0)),
        pl.BlockSpec((None, 16, HEAD_DIM), lambda b: (b, 0, 0)),
        pl.BlockSpec((None, SUBLANES, HEAD_DIM), lambda b: (b, 0, 0)),
        pl.BlockSpec((None, A_HEADS, A_DK, A_DV), lambda b: (b, 0, 0, 0)),
    ]
    out_shape = [
        jax.ShapeDtypeStruct((bs, A_HEADS, A_DV), F32),
        jax.ShapeDtypeStruct((bs, 16, HEAD_DIM), F32),
        jax.ShapeDtypeStruct((bs, SUBLANES, HEAD_DIM), F32),
        jax.ShapeDtypeStruct((bs, A_HEADS, A_DK, A_DV), F32),
    ]
    ya, ob, yc, sn = pl.pallas_call(
        _sample_body, grid=(bs,), in_specs=in_specs, out_specs=out_specs, out_shape=out_shape,
        compiler_params=_cparams(("parallel",), 32), name="sample_mixers",
    )(sink, q_rows, kn_rows, vn_rows, qk_a, v_a, g_a, lr, state_gla, cb, *cc, wlr_pad, blr, ng)
    return (ya.reshape(bs, A_HEADS * A_DV), ob[:, :B_HEADS].reshape(bs, B_HEADS * HEAD_DIM),
            yc[:, :C_HEADS].reshape(bs, C_HEADS * HEAD_DIM), sn)


def _outproj_body(*refs, merge):
    if merge:
        (ya_ref, ob_ref, o1_ref, l1_ref, o2_ref, l2_ref, o3_ref, l3_ref,
         wa_ref, wb_ref, wc_ref, bias_ref, h_ref, y_ref) = refs
        l1, l2, l3 = l1_ref[...], l2_ref[...], l3_ref[...]
        lmax = jnp.maximum(jnp.maximum(l1, l2), l3)
        e1, e2, e3 = jnp.exp(l1 - lmax), jnp.exp(l2 - lmax), jnp.exp(l3 - lmax)
        yc = (e1 * o1_ref[...] + e2 * o2_ref[...] + e3 * o3_ref[...]) / (e1 + e2 + e3)
    else:
        ya_ref, ob_ref, yc_ref, wa_ref, wb_ref, wc_ref, bias_ref, h_ref, y_ref = refs
        yc = yc_ref[...]
    acc = jnp.dot(ya_ref[...].astype(BF16), wa_ref[...], preferred_element_type=F32)
    acc = acc + jnp.dot(ob_ref[...].astype(BF16), wb_ref[...], preferred_element_type=F32)
    acc = acc + jnp.dot(yc.astype(BF16), wc_ref[...], preferred_element_type=F32)
    y_ref[...] = DN_ALPHA * h_ref[...] + acc + bias_ref[...]


def _outproj(ya, ob, yc_parts, wa, wb, wc, bias, h, *, tm, tn=1024):
    m = h.shape[0]
    merge = len(yc_parts) > 1

    def rowspec(width):
        return pl.BlockSpec((tm, width), lambda j, i: (i, 0))

    def wspec(rows):
        return pl.BlockSpec((rows, tn), lambda j, i: (0, j))

    in_specs = ([rowspec(ya.shape[1]), rowspec(ob.shape[1])] + [rowspec(x.shape[1]) for x in yc_parts]
                + [wspec(wa.shape[0]), wspec(wb.shape[0]), wspec(wc.shape[0]),
                   pl.BlockSpec((1, tn), lambda j, i: (0, j)), pl.BlockSpec((tm, tn), lambda j, i: (i, j))])
    return pl.pallas_call(
        functools.partial(_outproj_body, merge=merge), grid=(D_MODEL // tn, m // tm),
        in_specs=in_specs, out_specs=pl.BlockSpec((tm, tn), lambda j, i: (i, j)),
        out_shape=jax.ShapeDtypeStruct((m, D_MODEL), F32),
        compiler_params=_cparams(("parallel", "parallel"), 48), name="outproj",
    )(ya, ob, *yc_parts, wa, wb, wc, bias, h)


def _ln_router_body(y_ref, g_ref, b_ref, rw_ref, rb_ref, h_ref, hb_ref, lg_ref):
    h = _ln_rows(y_ref[...], g_ref[...], b_ref[...])
    h_ref[...] = h
    hb_ref[...] = h.astype(BF16)
    lg_ref[...] = jnp.dot(h, rw_ref[...], precision=HIGHEST, preferred_element_type=F32) + rb_ref[...]


def _ln_router(y, g, b, rw_pad, rb_pad, tm):
    m, d = y.shape
    row = pl.BlockSpec((tm, d), lambda i: (i, 0))
    vec = pl.BlockSpec((1, d), lambda i: (0, 0))
    return pl.pallas_call(
        _ln_router_body, grid=(m // tm,),
        in_specs=[row, vec, vec, pl.BlockSpec((d, LANES), lambda i: (0, 0)), pl.BlockSpec((1, LANES), lambda i: (0, 0))],
        out_specs=[row, row, pl.BlockSpec((tm, LANES), lambda i: (i, 0))],
        out_shape=[jax.ShapeDtypeStruct((m, d), F32), jax.ShapeDtypeStruct((m, d), BF16),
                   jax.ShapeDtypeStruct((m, LANES), F32)],
        compiler_params=_cparams(("parallel",), 48), name="ln_router",
    )(y, g, b, rw_pad, rb_pad)


def _moe_up_body(be_ref, na_ref, x_ref, wg_ref, wu_ref, bg_ref, bu_ref, a_ref, wg_bf, wu_bf):
    i = pl.program_id(1)
    fresh = (i == 0) | (be_ref[i] != be_ref[jnp.maximum(i - 1, 0)])
    active = i < na_ref[0]

    @pl.when(active & fresh)
    def _():
        wg_bf[...] = wg_ref[...].astype(BF16)
        wu_bf[...] = wu_ref[...].astype(BF16)

    @pl.when(active)
    def _():
        x = x_ref[...]
        g = jnp.dot(x, wg_bf[...], preferred_element_type=F32) + bg_ref[...]
        u = jnp.dot(x, wu_bf[...], preferred_element_type=F32) + bu_ref[...]
        g = jnp.minimum(g, SWIGLU_LIMIT)
        u = jnp.clip(u, -SWIGLU_LIMIT, SWIGLU_LIMIT)
        a_ref[...] = (g * jax.nn.sigmoid(SWIGLU_ALPHA * g) * (u + 1.0)).astype(BF16)


def _moe_down_body(be_ref, na_ref, a_ref, wd_ref, bd_ref, y_ref, wd_bf):
    i = pl.program_id(1)
    fresh = (i == 0) | (be_ref[i] != be_ref[jnp.maximum(i - 1, 0)])
    active = i < na_ref[0]

    @pl.when(active & fresh)
    def _():
        wd_bf[...] = wd_ref[...].astype(BF16)

    @pl.when(active)
    def _():
        y_ref[...] = jnp.dot(a_ref[...], wd_bf[...], preferred_element_type=F32) + bd_ref[...]


def _moe_experts(layer, xs, block_expert, n_active, w_gate, b_gate, w_up, b_up, w_down, b_down):
    m_pad, d = xs.shape
    d_ff = w_gate.shape[-1]
    n_blk = m_pad // MOE_TM
    tn = MOE_TN_UP

    def blk(i, na):
        return jnp.minimum(i, na[0] - 1)

    def wspec(k_dim, t):
        return pl.BlockSpec((None, None, k_dim, t), lambda j, i, be, na: (layer, be[i], 0, j))

    def bspec(t):
        return pl.BlockSpec((None, None, 1, t), lambda j, i, be, na: (layer, be[i], 0, j))

    act = pl.pallas_call(
        _moe_up_body,
        grid_spec=pltpu.PrefetchScalarGridSpec(
            num_scalar_prefetch=2, grid=(d_ff // tn, n_blk),
            in_specs=[pl.BlockSpec((MOE_TM, d), lambda j, i, be, na: (blk(i, na), 0)),
                      wspec(d, tn), wspec(d, tn), bspec(tn), bspec(tn)],
            out_specs=pl.BlockSpec((MOE_TM, tn), lambda j, i, be, na: (blk(i, na), j)),
            scratch_shapes=[pltpu.VMEM((d, tn), BF16), pltpu.VMEM((d, tn), BF16)]),
        out_shape=jax.ShapeDtypeStruct((m_pad, d_ff), BF16),
        compiler_params=_cparams(("arbitrary", "arbitrary"), 56), name="moe_gate_up",
    )(block_expert, n_active, xs, w_gate, w_up,
      b_gate.reshape(DEPTH, N_EXPERTS, 1, d_ff), b_up.reshape(DEPTH, N_EXPERTS, 1, d_ff))

    tn2 = MOE_TN_DOWN
    return pl.pallas_call(
        _moe_down_body,
        grid_spec=pltpu.PrefetchScalarGridSpec(
            num_scalar_prefetch=2, grid=(d // tn2, n_blk),
            in_specs=[pl.BlockSpec((MOE_TM, d_ff), lambda j, i, be, na: (blk(i, na), 0)),
                      wspec(d_ff, tn2), bspec(tn2)],
            out_specs=pl.BlockSpec((MOE_TM, tn2), lambda j, i, be, na: (blk(i, na), j)),
            scratch_shapes=[pltpu.VMEM((d_ff, tn2), BF16)]),
        out_shape=jax.ShapeDtypeStruct((m_pad, d), F32),
        compiler_params=_cparams(("arbitrary", "arbitrary"), 56), name="moe_down",
    )(block_expert, n_active, act, w_down, b_down.reshape(DEPTH, N_EXPERTS, 1, d))


def _combine_body(h_ref, y0_ref, y1_ref, y2_ref, y3_ref, gt_ref, g_ref, b_ref, o_ref, ob_ref):
    gt = gt_ref[...]
    moe = (y0_ref[...] * gt[:, 0:1] + y1_ref[...] * gt[:, 1:2]) + (y2_ref[...] * gt[:, 2:3] + y3_ref[...] * gt[:, 3:4])
    y = _ln_rows(DN_ALPHA * h_ref[...] + moe, g_ref[...], b_ref[...])
    o_ref[...] = y
    ob_ref[...] = y.astype(BF16)


def _combine_ln(h1, yg, gates, g, b, tm):
    m, d = h1.shape
    row = pl.BlockSpec((tm, d), lambda i: (i, 0))
    vec = pl.BlockSpec((1, d), lambda i: (0, 0))
    ysp = [pl.BlockSpec((None, tm, d), lambda i, kk=kk: (kk, i, 0)) for kk in range(TOP_K)]
    return pl.pallas_call(
        _combine_body, grid=(m // tm,),
        in_specs=[row] + ysp + [pl.BlockSpec((tm, TOP_K), lambda i: (i, 0)), vec, vec],
        out_specs=[row, row],
        out_shape=[jax.ShapeDtypeStruct((m, d), F32), jax.ShapeDtypeStruct((m, d), BF16)],
        compiler_params=_cparams(("parallel",), 48), name="combine_ln2",
    )(h1, yg, yg, yg, yg, gates, g, b)


def _route(logits):
    n_tok = logits.shape[0]
    top_v, top_e = lax.top_k(logits, TOP_K)
    gates = jax.nn.softmax(top_v, axis=-1)
    m = n_tok * TOP_K
    e_flat = top_e.reshape(m)
    n_blk = -(-m // MOE_TM) + N_EXPERTS
    onehot = jax.nn.one_hot(e_flat, N_EXPERTS, dtype=jnp.int32)
    counts = jnp.sum(onehot, axis=0)
    rank = jnp.take_along_axis(jnp.cumsum(onehot, axis=0) - onehot, e_flat[:, None], axis=1)[:, 0]
    blocks_per = (counts + MOE_TM - 1) // MOE_TM
    blocks_end = jnp.cumsum(blocks_per)
    dest = (blocks_end - blocks_per)[e_flat] * MOE_TM + rank
    n_active = blocks_end[-1]
    bidx = jnp.minimum(jnp.arange(n_blk), n_active - 1)
    block_expert = jnp.minimum(jnp.searchsorted(blocks_end, bidx, side="right"), N_EXPERTS - 1).astype(jnp.int32)
    src = jnp.zeros((n_blk * MOE_TM,), jnp.int32).at[dest].set(jnp.arange(m, dtype=jnp.int32) // TOP_K)
    return gates, dest, src, block_expert, n_active.reshape(1).astype(jnp.int32)


def _rope_tables(pos):
    half = HEAD_DIM // 2
    inv_freq = ROPE_THETA ** (-np.arange(half, dtype=np.float64) / half)
    ang = np.asarray(pos, np.float64)[:, None] * inv_freq[None, :]
    cos, sin = np.cos(ang), np.sin(ang)
    return (jnp.asarray(np.concatenate([cos, cos], axis=1), F32),
            jnp.asarray(np.concatenate([-sin, sin], axis=1), F32))


def _split_w_in(w, b):
    k = w.shape[0]
    w_r = jnp.concatenate([w[:, _O_QB:_O_VB], w[:, _O_QC:_O_VC], jnp.zeros((k, LANES), w.dtype)], axis=1)
    w_p = jnp.concatenate([w[:, _O_QA:_O_LR], w[:, _O_VB:_O_QC], w[:, _O_VC:_O_END], w[:, _O_LR:_O_QB],
                           jnp.zeros((k, LANES - A_LOWRANK), w.dtype)], axis=1)
    b_r = jnp.concatenate([b[_O_QB:_O_VB], b[_O_QC:_O_VC], jnp.zeros((LANES,), b.dtype)])
    b_p = jnp.concatenate([b[_O_QA:_O_LR], b[_O_VB:_O_QC], b[_O_VC:_O_END], b[_O_LR:_O_QB],
                           jnp.zeros((LANES - A_LOWRANK,), b.dtype)])
    return w_r.astype(BF16), w_p.astype(BF16), b_r.reshape(1, W_PROJ), b_p.reshape(1, W_PROJ)


def _mix_prompt(rp, pp, wlr_pad, blr, ng, sink, bp, tlen):
    ya, st = _gla_prompt(pp, wlr_pad, blr, ng, bp, tlen)
    (ob,) = _band_attention(rp, pp, sink, bsz=bp, tlen=tlen, dil=1, heads=B_KV_HEADS, group=B_GROUP,
                            uq=R_QB, uk=R_KB, uv=P_VB, want_lse=False)
    yc_parts = []
    for g, (win, dil) in enumerate(C_PATTERNS):
        assert win // dil == ATTN_BLOCK
        yc_parts += _band_attention(rp, pp, None, bsz=bp, tlen=tlen, dil=dil, heads=C_HEADS, group=1,
                                    uq=R_QC + C_HEADS * g, uk=R_KC + C_HEADS * g, uv=P_VC + C_HEADS * g,
                                    want_lse=True)
    return ya, ob, yc_parts, st


def kernel(x_prompt, x_sample, cache_b_kv, cache_c1_kv, cache_c2_kv, cache_c3_kv, state_gla, ln_in_g, ln_in_b, w_in, b_in, gla_w_gate, gla_b_gate, gla_norm_g, attn_sinks, w_out, b_out, ln1_g, ln1_b, router_w, router_b, w_gate, b_gate, w_up, b_up, w_down, b_down, ln2_g, ln2_b):
    bp, tlen, d = x_prompt.shape
    bs, slen, _ = x_sample.shape
    assert slen == 1 and d == D_MODEL and tlen % (16 * ATTN_BLOCK) == 0
    n_p, n_s = bp * tlen, bs * slen
    tm_p = 256

    cos_p, sin_p = _rope_tables(np.tile(np.arange(tlen), bp))
    cos_s, sin_s = _rope_tables(np.full((n_s,), PAST_LEN))
    caches_c = (cache_c1_kv, cache_c2_kv, cache_c3_kv)

    hp, hp_b = _ln_in(x_prompt.reshape(n_p, d), ln_in_g, ln_in_b, tm_p)
    hs, hs_b = _ln_in(x_sample.reshape(n_s, d), ln_in_g, ln_in_b, n_s)

    new_p = [[] for _ in range(5)]
    new_s = [[] for _ in range(5)]
    for l in range(DEPTH):
        w_r, w_p, b_r, b_p = _split_w_in(w_in[l], b_in[l])
        wlr_pad = jnp.concatenate([gla_w_gate[l], jnp.zeros((LANES - A_LOWRANK, A_HEADS * A_DK), F32)], axis=0)
        blr = gla_b_gate[l].reshape(1, A_HEADS * A_DK)
        ng = gla_norm_g[l].reshape(1, A_DV)
        sink = attn_sinks[l].reshape(B_HEADS)
        wo = w_out[l].astype(BF16)
        na, nb = A_HEADS * A_DV, B_HEADS * HEAD_DIM
        wo_a, wo_b, wo_c = wo[:na], wo[na:na + nb], wo[na + nb:]
        bo = b_out[l].reshape(1, d)

        rp = _proj(hp_b, w_r, b_r, cos_p, sin_p, tm=512)
        pp = _proj(hp_b, w_p, b_p, tm=512)
        rs = _proj(hs_b, w_r, b_r, cos_s, sin_s, tm=n_s)
        ps = _proj(hs_b, w_p, b_p, tm=n_s)

        ya_p, ob_p, yc_parts, st_p = _mix_prompt(rp, pp, wlr_pad, blr, ng, sink, bp, tlen)
        ya_s, ob_s, yc_s, st_s = _sample_mixers(l, rs, ps, state_gla, cache_b_kv, caches_c, sink, wlr_pad, blr, ng)

        rp4 = rp.reshape(bp, tlen, W_PROJ)
        pp4 = pp.reshape(bp, tlen, W_PROJ)

        def kv_prompt(uk, uv, heads, rows):
            kk = rp4[:, tlen - rows:, uk * LANES:(uk + heads) * LANES].reshape(bp, rows, heads, HEAD_DIM)
            vv = pp4[:, tlen - rows:, uv * LANES:(uv + heads) * LANES].reshape(bp, rows, heads, HEAD_DIM)
            return jnp.stack([kk, vv], axis=1)

        def kv_sample(uk, uv, heads):
            kk = rs[:, uk * LANES:(uk + heads) * LANES].reshape(bs, slen, heads, HEAD_DIM)
            vv = ps[:, uv * LANES:(uv + heads) * LANES].reshape(bs, slen, heads, HEAD_DIM)
            return jnp.stack([kk, vv], axis=1)

        new_p[0].append(kv_prompt(R_KB, P_VB, B_KV_HEADS, min(B_WINDOW, tlen)))
        new_s[0].append(kv_sample(R_KB, P_VB, B_KV_HEADS))
        for g, (win, dil) in enumerate(C_PATTERNS):
            new_p[1 + g].append(kv_prompt(R_KC + C_HEADS * g, P_VC + C_HEADS * g, C_HEADS, min(win, tlen)))
            new_s[1 + g].append(kv_sample(R_KC + C_HEADS * g, P_VC + C_HEADS * g, C_HEADS))
        new_p[4].append(st_p)
        new_s[4].append(st_s)

        y_p = _outproj(ya_p, ob_p, yc_parts, wo_a, wo_b, wo_c, bo, hp, tm=256)
        y_s = _outproj(ya_s, ob_s, [yc_s], wo_a, wo_b, wo_c, bo, hs, tm=n_s)
        rw_pad = jnp.concatenate([router_w[l], jnp.zeros((d, LANES - N_EXPERTS), F32)], axis=1)
        rb_pad = jnp.concatenate([router_b[l], jnp.zeros((LANES - N_EXPERTS,), F32)]).reshape(1, LANES)
        g1, b1 = ln1_g[l].reshape(1, d), ln1_b[l].reshape(1, d)
        h1_p, h1b_p, lg_p = _ln_router(y_p, g1, b1, rw_pad, rb_pad, tm_p)
        h1_s, h1b_s, lg_s = _ln_router(y_s, g1, b1, rw_pad, rb_pad, n_s)

        logits = jnp.concatenate([lg_p, lg_s], axis=0)[:, :N_EXPERTS]
        gates, dest, src, block_expert, n_active = _route(logits)
        h1b = jnp.concatenate([h1b_p, h1b_s], axis=0)
        xs = h1b[src]
        yb = _moe_experts(l, xs, block_expert, n_active, w_gate, b_gate, w_up, b_up, w_down, b_down)
        dest_t = dest.reshape(n_p + n_s, TOP_K).T
        g2, b2 = ln2_g[l].reshape(1, d), ln2_b[l].reshape(1, d)
        hp, hp_b = _combine_ln(h1_p, yb[dest_t[:, :n_p]], gates[:n_p], g2, b2, 128)
        hs, hs_b = _combine_ln(h1_s, yb[dest_t[:, n_p:]], gates[n_p:], g2, b2, n_s)

    outs = [hp.reshape(bp, tlen, d), hs.reshape(bs, slen, d)]
    for i in range(5):
        outs += [jnp.stack(new_p[i], axis=1), jnp.stack(new_s[i], axis=1)]
    return tuple(outs)
```

```python
import functools

import numpy as np
import jax
import jax.numpy as jnp
from jax import lax
from jax.experimental import pallas as pl
from jax.experimental.pallas import tpu as pltpu

F32 = jnp.float32
BF16 = jnp.bfloat16
HIGHEST = lax.Precision.HIGHEST

LANES = 128
SUBLANES = 8

D_MODEL = 4096
DEPTH = 2
PAST_LEN = 16384
HEAD_DIM = 128
A_HEADS, A_DK, A_DV, A_LOWRANK = 4, 128, 256, 16
A_GATE_NORM = 16.0
A_CHUNK = 64
B_HEADS, B_KV_HEADS = 12, 3
B_GROUP = B_HEADS // B_KV_HEADS
B_WINDOW = 128
C_PATTERNS = ((128, 1), (512, 4), (2048, 16))
N_C_GROUPS = len(C_PATTERNS)
C_HEADS = 4
ATTN_BLOCK = 128
ROPE_THETA = 10000.0
N_EXPERTS = 32
TOP_K = 4
SWIGLU_LIMIT = 7.0
SWIGLU_ALPHA = 1.702
DN_ALPHA = (2 * DEPTH) ** 0.25
LN_EPS = 1e-5
RMS_EPS = 1e-6
NEG_INF = -1e30
ATTN_SCALE = HEAD_DIM ** -0.5

_O_QA, _O_KA, _O_VA, _O_GA, _O_LR, _O_QB, _O_KB, _O_VB, _O_QC, _O_KC, _O_VC, _O_END = (
    0, 512, 1024, 2048, 3072, 3088, 4624, 5008, 5392, 6928, 8464, 10000)
R_QB, R_KB, R_QC, R_KC = 0, 12, 15, 27
P_QA, P_KA, P_VA, P_GA, P_VB, P_VC, P_LR = 0, 4, 8, 16, 24, 27, 39
U_PROJ = 40
W_PROJ = U_PROJ * LANES

S_ROWS = 16 + SUBLANES * N_C_GROUPS

MOE_TM = 256
MOE_TN_UP = 512
MOE_TN_DOWN = 1024
SWA_ROWS = 512


def _cparams(sem, vmem_mb):
    return pltpu.CompilerParams(dimension_semantics=sem, vmem_limit_bytes=vmem_mb * 1024 * 1024)


def _ln_rows(x, g, b):
    mu = jnp.mean(x, axis=-1, keepdims=True)
    xc = x - mu
    var = jnp.mean(xc * xc, axis=-1, keepdims=True)
    return xc * lax.rsqrt(var + LN_EPS) * g + b


def _ln_in_body(x_ref, g_ref, b_ref, o_ref, ob_ref):
    y = _ln_rows(x_ref[...], g_ref[...], b_ref[...])
    o_ref[...] = y
    ob_ref[...] = y.astype(BF16)


def _ln_in(x, g, b, tm):
    m, d = x.shape
    row = pl.BlockSpec((tm, d), lambda i: (i, 0))
    vec = pl.BlockSpec((1, d), lambda i: (0, 0))
    return pl.pallas_call(
        _ln_in_body, grid=(m // tm,), in_specs=[row, vec, vec], out_specs=[row, row],
        out_shape=[jax.ShapeDtypeStruct((m, d), F32), jax.ShapeDtypeStruct((m, d), BF16)],
        compiler_params=_cparams(("parallel",), 48), name="ln_in",
    )(x, g.reshape(1, d), b.reshape(1, d))


def _proj_body(*refs, rope, tn):
    if rope:
        x_ref, w_ref, b_ref, cos_ref, sin_ref, o_ref = refs
    else:
        x_ref, w_ref, b_ref, o_ref = refs
    acc = jnp.dot(x_ref[...], w_ref[...], preferred_element_type=F32) + b_ref[...]
    if not rope:
        o_ref[...] = acc
        return
    cos = cos_ref[...]
    sin = sin_ref[...]
    for u in range(tn // LANES):
        seg = acc[:, u * LANES:(u + 1) * LANES]
        o_ref[:, u * LANES:(u + 1) * LANES] = seg * cos + pltpu.roll(seg, HEAD_DIM // 2, 1) * sin


def _proj(xb, w, b, cos=None, sin=None, *, tm, tn=1024):
    m, k = xb.shape
    n = w.shape[1]
    rope = cos is not None
    in_specs = [pl.BlockSpec((tm, k), lambda j, i: (i, 0)),
                pl.BlockSpec((k, tn), lambda j, i: (0, j)),
                pl.BlockSpec((1, tn), lambda j, i: (0, j))]
    args = [xb, w, b]
    if rope:
        tab = pl.BlockSpec((tm, LANES), lambda j, i: (i, 0))
        in_specs += [tab, tab]
        args += [cos, sin]
    return pl.pallas_call(
        functools.partial(_proj_body, rope=rope, tn=tn), grid=(n // tn, m // tm),
        in_specs=in_specs, out_specs=pl.BlockSpec((tm, tn), lambda j, i: (i, j)),
        out_shape=jax.ShapeDtypeStruct((m, n), F32),
        compiler_params=_cparams(("parallel", "parallel"), 48),
        name="proj_rope" if rope else "proj_plain",
    )(*args)


def _log_sigmoid(z):
    return jnp.minimum(z, 0.0) - jnp.log(1.0 + jnp.exp(-jnp.abs(z)))


def _rms_gate(o, ng, ga):
    o = o * lax.rsqrt(jnp.mean(o * o, axis=-1, keepdims=True) + RMS_EPS) * ng
    return o * (ga * jax.nn.sigmoid(ga))


def _gla_body(q_ref, k_ref, v_ref, ga_ref, lr_ref, wlr_ref, blr_ref, ng_ref, ya_ref, st_ref, s_scr, *, n_sub):
    c = pl.program_id(2)

    @pl.when(c == 0)
    def _():
        s_scr[...] = jnp.zeros_like(s_scr)

    ch = A_CHUNK
    rr = lax.broadcasted_iota(jnp.int32, (ch, ch), 0)
    cc = lax.broadcasted_iota(jnp.int32, (ch, ch), 1)
    causal = rr >= cc
    tri = causal.astype(F32)
    ones = jnp.ones((ch, A_DK), F32)
    for u in range(n_sub):
        sl = pl.ds(u * ch, ch)
        z = jnp.dot(lr_ref[sl, :], wlr_ref[...], precision=HIGHEST, preferred_element_type=F32) + blr_ref[...]
        g = _log_sigmoid(z) / A_GATE_NORM
        b = jnp.dot(tri, g, precision=HIGHEST, preferred_element_type=F32)
        b_last = b[ch - 1:ch, :]
        b_last_col = lax.dot_general(g, ones, (((0,), (0,)), ((), ())), precision=HIGHEST,
                                     preferred_element_type=F32)
        q = q_ref[sl, :] * (A_DK ** -0.5)
        k = k_ref[sl, :]
        v = v_ref[sl, :].astype(BF16)
        qg = (q * jnp.exp(b)).astype(BF16)
        kg = (k * jnp.exp(-b)).astype(BF16)
        kd = (k * jnp.exp(b_last - b)).astype(BF16)
        att = lax.dot_general(qg, kg, (((1,), (1,)), ((), ())), preferred_element_type=F32)
        att = jnp.where(causal, att, 0.0).astype(BF16)
        s = s_scr[...]
        o = (jnp.dot(qg, s.astype(BF16), preferred_element_type=F32)
             + jnp.dot(att, v, preferred_element_type=F32))
        decay = jnp.exp(b_last_col)
        decay = jnp.concatenate([decay, decay], axis=1)
        s_scr[...] = decay * s + lax.dot_general(kd, v, (((0,), (0,)), ((), ())), preferred_element_type=F32)
        ya_ref[sl, :] = _rms_gate(o, ng_ref[...], ga_ref[sl, :])

    @pl.when(c == pl.num_programs(2) - 1)
    def _():
        st_ref[...] = s_scr[...]


def _gla_prompt(p_arr, wlr_pad, blr, ng, bsz, tlen, rb=256):
    nblk = tlen // rb
    n_sub = rb // A_CHUNK

    def rows(b, c):
        return b * nblk + c

    in_specs = [
        pl.BlockSpec((rb, A_DK), lambda b, h, c: (rows(b, c), P_QA + h)),
        pl.BlockSpec((rb, A_DK), lambda b, h, c: (rows(b, c), P_KA + h)),
        pl.BlockSpec((rb, A_DV), lambda b, h, c: (rows(b, c), P_VA // 2 + h)),
        pl.BlockSpec((rb, A_DV), lambda b, h, c: (rows(b, c), P_GA // 2 + h)),
        pl.BlockSpec((rb, LANES), lambda b, h, c: (rows(b, c), P_LR)),
        pl.BlockSpec((LANES, A_DK), lambda b, h, c: (0, h)),
        pl.BlockSpec((1, A_DK), lambda b, h, c: (0, h)),
        pl.BlockSpec((1, A_DV), lambda b, h, c: (0, 0)),
    ]
    out_specs = [
        pl.BlockSpec((rb, A_DV), lambda b, h, c: (rows(b, c), h)),
        pl.BlockSpec((None, None, A_DK, A_DV), lambda b, h, c: (b, h, 0, 0)),
    ]
    return pl.pallas_call(
        functools.partial(_gla_body, n_sub=n_sub), grid=(bsz, A_HEADS, nblk),
        in_specs=in_specs, out_specs=out_specs,
        out_shape=[jax.ShapeDtypeStruct((bsz * tlen, A_HEADS * A_DV), F32),
                   jax.ShapeDtypeStruct((bsz, A_HEADS, A_DK, A_DV), F32)],
        scratch_shapes=[pltpu.VMEM((A_DK, A_DV), F32)],
        compiler_params=_cparams(("parallel", "parallel", "arbitrary"), 32), name="gla_prompt",
    )(p_arr, p_arr, p_arr, p_arr, p_arr, wlr_pad, blr, ng)


def _attn_block(q16, k_prev, k_cur, v_prev, v_cur, prev_on, sink):
    blk = ATTN_BLOCK
    rows = q16.shape[0]
    nk = blk if k_prev is None else 2 * blk
    r = lax.broadcasted_iota(jnp.int32, (rows, nk), 0)
    if rows > blk:
        r = r % blk
    c = lax.broadcasted_iota(jnp.int32, (rows, nk), 1)
    if k_prev is None:
        keys, vals = k_cur.astype(BF16), v_cur.astype(BF16)
        mask = c <= r
    else:
        keys = jnp.concatenate([k_prev, k_cur], axis=0).astype(BF16)
        vals = jnp.concatenate([v_prev, v_cur], axis=0).astype(BF16)
        in_prev = (c < blk) & (c >= r)
        if prev_on is not True:
            in_prev = in_prev & prev_on
        mask = in_prev | ((c >= blk) & (c - blk <= r))
    s = lax.dot_general(q16, keys, (((1,), (1,)), ((), ())), preferred_element_type=F32) * ATTN_SCALE
    s = jnp.where(mask, s, NEG_INF)
    m = jnp.max(s, axis=1, keepdims=True)
    if sink is not None:
        m = jnp.maximum(m, sink)
    p = jnp.exp(s - m)
    den = jnp.sum(p, axis=1, keepdims=True)
    if sink is not None:
        den = den + jnp.exp(sink - m)
    o = jnp.dot(p.astype(BF16), vals, preferred_element_type=F32) / den
    return o, m + jnp.log(den)


def _swa_body(sink_ref, q_ref, kc_ref, vc_ref, kp_ref, vp_ref, o_ref):
    blk = ATTN_BLOCK
    hk = pl.program_id(1)
    n = pl.program_id(2)
    sink = jnp.concatenate([jnp.full((blk, 1), sink_ref[hk * B_GROUP + g], F32) for g in range(B_GROUP)], axis=0)
    for i in range(SWA_ROWS // blk):
        rows = slice(i * blk, (i + 1) * blk)
        q = q_ref[rows, :]
        q16 = jnp.concatenate([q[:, g * HEAD_DIM:(g + 1) * HEAD_DIM] for g in range(B_GROUP)], axis=0).astype(BF16)
        if i == 0:
            k_prev, v_prev, prev_on = kp_ref[...], vp_ref[...], n > 0
        else:
            prev = slice((i - 1) * blk, i * blk)
            k_prev, v_prev, prev_on = kc_ref[prev, :], vc_ref[prev, :], True
        o, _ = _attn_block(q16, k_prev, kc_ref[rows, :], v_prev, vc_ref[rows, :], prev_on, sink)
        o_ref[rows, :] = jnp.concatenate([o[g * blk:(g + 1) * blk, :] for g in range(B_GROUP)], axis=1)


def _swa_attention(r_arr, p_arr, sink, bsz, tlen):
    nsb = tlen // SWA_ROWS
    sub = SWA_ROWS // ATTN_BLOCK
    gw = B_GROUP * HEAD_DIM

    def prev_blk(b, n):
        return (b * nsb + n) * sub - jnp.minimum(n, 1)

    in_specs = [
        pl.BlockSpec(memory_space=pltpu.SMEM),
        pl.BlockSpec((SWA_ROWS, gw), lambda b, h, n: (b * nsb + n, R_QB // B_GROUP + h)),
        pl.BlockSpec((SWA_ROWS, HEAD_DIM), lambda b, h, n: (b * nsb + n, R_KB + h)),
        pl.BlockSpec((SWA_ROWS, HEAD_DIM), lambda b, h, n: (b * nsb + n, P_VB + h)),
        pl.BlockSpec((ATTN_BLOCK, HEAD_DIM), lambda b, h, n: (prev_blk(b, n), R_KB + h)),
        pl.BlockSpec((ATTN_BLOCK, HEAD_DIM), lambda b, h, n: (prev_blk(b, n), P_VB + h)),
    ]
    return pl.pallas_call(
        _swa_body, grid=(bsz, B_KV_HEADS, nsb), in_specs=in_specs,
        out_specs=pl.BlockSpec((SWA_ROWS, gw), lambda b, h, n: (b * nsb + n, h)),
        out_shape=jax.ShapeDtypeStruct((bsz * tlen, B_HEADS * HEAD_DIM), F32),
        compiler_params=_cparams(("parallel", "parallel", "parallel"), 32), name="swa_attn",
    )(sink, r_arr, r_arr, p_arr, r_arr, p_arr)


def _dilated_body(*refs, tlen):
    qkv = refs[:3 * N_C_GROUPS]
    yc_ref = refs[3 * N_C_GROUPS]
    o_scr = refs[3 * N_C_GROUPS + 1:3 * N_C_GROUPS + 1 + N_C_GROUPS]
    l_scr = refs[3 * N_C_GROUPS + 1 + N_C_GROUPS:]
    blk = ATTN_BLOCK
    for g, (win, dil) in enumerate(C_PATTERNS):
        q_ref, k_ref, v_ref = qkv[3 * g:3 * g + 3]
        for res in range(dil):
            for i in range(tlen // dil // blk):
                def rows(ii):
                    start = res + dil * blk * ii
                    return pl.ds(start, blk) if dil == 1 else pl.ds(start, blk, stride=dil)
                cur = rows(i)
                if i == 0:
                    k_prev = v_prev = None
                else:
                    k_prev, v_prev = k_ref[rows(i - 1), :], v_ref[rows(i - 1), :]
                o, lse = _attn_block(q_ref[cur, :].astype(BF16), k_prev, k_ref[cur, :], v_prev, v_ref[cur, :], True, None)
                o_scr[g][cur, :] = o
                l_scr[g][cur, :] = jnp.broadcast_to(lse, (blk, HEAD_DIM))
    step = 2 * blk
    for ch in range(tlen // step):
        rows = slice(ch * step, (ch + 1) * step)
        ls = [l[rows, :] for l in l_scr]
        lmax = jnp.maximum(jnp.maximum(ls[0], ls[1]), ls[2])
        es = [jnp.exp(l - lmax) for l in ls]
        num = es[0] * o_scr[0][rows, :] + es[1] * o_scr[1][rows, :] + es[2] * o_scr[2][rows, :]
        yc_ref[rows, :] = num / (es[0] + es[1] + es[2])


def _dilated_attention(r_arr, p_arr, bsz, tlen):
    in_specs, args = [], []
    for g, (win, dil) in enumerate(C_PATTERNS):
        assert win // dil == ATTN_BLOCK and tlen % (dil * ATTN_BLOCK) == 0
        in_specs += [pl.BlockSpec((tlen, HEAD_DIM), lambda b, h, g=g: (b, R_QC + C_HEADS * g + h)),
                     pl.BlockSpec((tlen, HEAD_DIM), lambda b, h, g=g: (b, R_KC + C_HEADS * g + h)),
                     pl.BlockSpec((tlen, HEAD_DIM), lambda b, h, g=g: (b, P_VC + C_HEADS * g + h))]
        args += [r_arr, r_arr, p_arr]
    return pl.pallas_call(
        functools.partial(_dilated_body, tlen=tlen), grid=(bsz, C_HEADS), in_specs=in_specs,
        out_specs=pl.BlockSpec((tlen, HEAD_DIM), lambda b, h: (b, h)),
        out_shape=jax.ShapeDtypeStruct((bsz * tlen, C_HEADS * HEAD_DIM), F32),
        scratch_shapes=[pltpu.VMEM((tlen, HEAD_DIM), F32) for _ in range(2 * N_C_GROUPS)],
        compiler_params=_cparams(("parallel", "parallel"), 40), name="dilated_attn",
    )(*args)


def _sample_body(sink_ref, q_ref, kn_ref, vn_ref, qk_ref, va_ref, ga_ref, lr_ref, st_ref,
                 cb_ref, c1_ref, c2_ref, c3_ref, wlr_ref, blr_ref, ng_ref,
                 ya_ref, ob_ref, yc_ref, sn_ref):
    nt = (((1,), (1,)), ((), ()))

    z = jnp.dot(lr_ref[...], wlr_ref[...], precision=HIGHEST, preferred_element_type=F32)[0:1, :] + blr_ref[...]
    a_all = jnp.exp(_log_sigmoid(z) / A_GATE_NORM)
    qk = qk_ref[...]
    rid16 = lax.broadcasted_iota(jnp.int32, (16, 1), 0)
    rows = jnp.zeros((16, A_DK), F32)
    for h in range(A_HEADS):
        a_h = jnp.broadcast_to(a_all[:, h * A_DK:(h + 1) * A_DK], (16, A_DK))
        k_h = jnp.broadcast_to(qk[A_HEADS + h:A_HEADS + h + 1, :], (16, A_DK))
        q_h = jnp.broadcast_to(qk[h:h + 1, :], (16, A_DK)) * (A_DK ** -0.5)
        rows = jnp.where(rid16 == 3 * h, a_h, rows)
        rows = jnp.where(rid16 == 3 * h + 1, k_h, rows)
        rows = jnp.where(rid16 == 3 * h + 2, q_h, rows)
    cols = jnp.concatenate([rows, jnp.zeros((LANES - 16, A_DK), F32)], axis=0).T
    for h in range(A_HEADS):
        a_col, k_col, q_col = cols[:, 3 * h:3 * h + 1], cols[:, 3 * h + 1:3 * h + 2], cols[:, 3 * h + 2:3 * h + 3]
        s_new = a_col * st_ref[h] + k_col * va_ref[h:h + 1, :]
        sn_ref[h] = s_new
        o = jnp.sum(q_col * s_new, axis=0, keepdims=True)
        ya_ref[h:h + 1, :] = _rms_gate(o, ng_ref[...], ga_ref[h:h + 1, :])

    q = q_ref[...]
    q16 = q.astype(BF16)
    rid = lax.broadcasted_iota(jnp.int32, (S_ROWS, 1), 0)

    def pick_b(j):
        return (rid < B_HEADS) & (rid // B_GROUP == j)

    def pick_c(g, hh):
        return rid == 16 + SUBLANES * g + hh

    c_refs = (c1_ref, c2_ref, c3_ref)
    s = jnp.zeros((S_ROWS, ATTN_BLOCK), F32)
    for j in range(B_KV_HEADS):
        kj = cb_ref[0, :, j * HEAD_DIM:(j + 1) * HEAD_DIM].astype(BF16)
        s = jnp.where(pick_b(j), lax.dot_general(q16, kj, nt, preferred_element_type=F32), s)
    for g in range(N_C_GROUPS):
        for hh in range(C_HEADS):
            kh = c_refs[g][0, :, hh * HEAD_DIM:(hh + 1) * HEAD_DIM].astype(BF16)
            s = jnp.where(pick_c(g, hh), lax.dot_general(q16, kh, nt, preferred_element_type=F32), s)
    s = s * ATTN_SCALE
    s_n = jnp.sum(q * kn_ref[...], axis=1, keepdims=True) * ATTN_SCALE
    sink = jnp.full((S_ROWS, 1), NEG_INF, F32)
    for i in range(B_HEADS):
        sink = jnp.where(rid == i, sink_ref[i], sink)
    m = jnp.maximum(jnp.maximum(jnp.max(s, axis=1, keepdims=True), s_n), sink)
    p = jnp.exp(s - m)
    p_n = jnp.exp(s_n - m)
    den = jnp.sum(p, axis=1, keepdims=True) + p_n + jnp.exp(sink - m)
    p16 = p.astype(BF16)
    acc = p_n * vn_ref[...]
    for j in range(B_KV_HEADS):
        vj = cb_ref[1, :, j * HEAD_DIM:(j + 1) * HEAD_DIM].astype(BF16)
        acc = acc + jnp.where(pick_b(j), jnp.dot(p16, vj, preferred_element_type=F32), 0.0)
    for g in range(N_C_GROUPS):
        for hh in range(C_HEADS):
            vh = c_refs[g][1, :, hh * HEAD_DIM:(hh + 1) * HEAD_DIM].astype(BF16)
            acc = acc + jnp.where(pick_c(g, hh), jnp.dot(p16, vh, preferred_element_type=F32), 0.0)
    o = acc / den
    lse = m + jnp.log(den)
    ob_ref[...] = o[0:16, :]
    ls = [lse[16 + SUBLANES * g:24 + SUBLANES * g, :] for g in range(N_C_GROUPS)]
    os_ = [o[16 + SUBLANES * g:24 + SUBLANES * g, :] for g in range(N_C_GROUPS)]
    lmax = jnp.maximum(jnp.maximum(ls[0], ls[1]), ls[2])
    es = [jnp.exp(l - lmax) for l in ls]
    yc_ref[...] = (es[0] * os_[0] + es[1] * os_[1] + es[2] * os_[2]) / (es[0] + es[1] + es[2])


def _sample_mixers(layer, r_s, p_s, state_gla, cache_b, caches_c, sink, wlr_pad, blr, ng):
    bs = r_s.shape[0]
    r3 = r_s.reshape(bs, U_PROJ, LANES)
    p3 = p_s.reshape(bs, U_PROJ, LANES)
    z4 = jnp.zeros((bs, 4, LANES), F32)

    def qrows(x3, ub, uc, rep):
        parts = [jnp.repeat(x3[:, ub:ub + B_KV_HEADS], B_GROUP, axis=1) if rep else x3[:, ub:ub + B_HEADS], z4]
        for g in range(N_C_GROUPS):
            parts += [x3[:, uc + C_HEADS * g:uc + C_HEADS * (g + 1)], z4]
        return jnp.concatenate(parts, axis=1)

    q_rows = qrows(r3, R_QB, R_QC, False)
    kn_rows = qrows(r3, R_KB, R_KC, True)
    vn_rows = qrows(p3, P_VB, P_VC, True)
    qk_a = p3[:, P_QA:P_VA]
    v_a = p_s[:, P_VA * LANES:P_GA * LANES].reshape(bs, A_HEADS, A_DV)
    g_a = p_s[:, P_GA * LANES:P_VB * LANES].reshape(bs, A_HEADS, A_DV)
    lr = jnp.broadcast_to(p3[:, P_LR:P_LR + 1], (bs, SUBLANES, LANES))

    cb = cache_b.reshape(bs, DEPTH, 2, B_WINDOW, B_KV_HEADS * HEAD_DIM)
    cc = [c.reshape(bs, DEPTH, 2, ATTN_BLOCK, dil * C_HEADS * HEAD_DIM) for c, (_, dil) in zip(caches_c, C_PATTERNS)]

    def tok(rows, width):
        return pl.BlockSpec((None, rows, width), lambda b: (b, 0, 0))

    in_specs = [
        pl.BlockSpec(memory_space=pltpu.SMEM),
        tok(S_ROWS, LANES), tok(S_ROWS, LANES), tok(S_ROWS, LANES),
        tok(2 * A_HEADS, LANES), tok(A_HEADS, A_DV), tok(A_HEADS, A_DV), tok(SUBLANES, LANES),
        pl.BlockSpec((None, None, A_HEADS, A_DK, A_DV), lambda b: (b, layer, 0, 0, 0)),
        pl.BlockSpec((None, None, 2, B_WINDOW, B_KV_HEADS * HEAD_DIM), lambda b: (b, layer, 0, 0, 0)),
    ] + [pl.BlockSpec((None, None, 2, ATTN_BLOCK, C_HEADS * HEAD_DIM), lambda b: (b, layer, 0, 0, 0)) for _ in cc] + [
        pl.BlockSpec((LANES, A_HEADS * A_DK), lambda b: (0, 0)),
        pl.BlockSpec((1, A_HEADS * A_DK), lambda b: (0, 0)),
        pl.BlockSpec((1, A_DV), lambda b: (0, 0)),
    ]
    out_specs = [
        pl.BlockSpec((None, A_HEADS, A_DV), lambda b: (b, 0, 0)),
        pl.BlockSpec((None, 16, HEAD_DIM), lambda b: (b, 0, 0)),
        pl.BlockSpec((None, SUBLANES, HEAD_DIM), lambda b: (b, 0, 0)),
        pl.BlockSpec((None, A_HEADS, A_DK, A_DV), lambda b: (b, 0, 0, 0)),
    ]
    out_shape = [
        jax.ShapeDtypeStruct((bs, A_HEADS, A_DV), F32),
        jax.ShapeDtypeStruct((bs, 16, HEAD_DIM), F32),
        jax.ShapeDtypeStruct((bs, SUBLANES, HEAD_DIM), F32),
        jax.ShapeDtypeStruct((bs, A_HEADS, A_DK, A_DV), F32),
    ]
    ya, ob, yc, sn = pl.pallas_call(
        _sample_body, grid=(bs,), in_specs=in_specs, out_specs=out_specs, out_shape=out_shape,
        compiler_params=_cparams(("parallel",), 32), name="sample_mixers",
    )(sink, q_rows, kn_rows, vn_rows, qk_a, v_a, g_a, lr, state_gla, cb, *cc, wlr_pad, blr, ng)
    return (ya.reshape(bs, A_HEADS * A_DV), ob[:, :B_HEADS].reshape(bs, B_HEADS * HEAD_DIM),
            yc[:, :C_HEADS].reshape(bs, C_HEADS * HEAD_DIM), sn)


def _outproj_body(ya_ref, ob_ref, yc_ref, wa_ref, wb_ref, wc_ref, bias_ref, h_ref, y_ref):
    acc = jnp.dot(ya_ref[...].astype(BF16), wa_ref[...], preferred_element_type=F32)
    acc = acc + jnp.dot(ob_ref[...].astype(BF16), wb_ref[...], preferred_element_type=F32)
    acc = acc + jnp.dot(yc_ref[...].astype(BF16), wc_ref[...], preferred_element_type=F32)
    y_ref[...] = DN_ALPHA * h_ref[...] + acc + bias_ref[...]


def _outproj(ya, ob, yc, wa, wb, wc, bias, h, *, tm, tn=1024):
    m = h.shape[0]

    def rowspec(x):
        return pl.BlockSpec((tm, x.shape[1]), lambda j, i: (i, 0))

    def wspec(w):
        return pl.BlockSpec((w.shape[0], tn), lambda j, i: (0, j))

    return pl.pallas_call(
        _outproj_body, grid=(D_MODEL // tn, m // tm),
        in_specs=[rowspec(ya), rowspec(ob), rowspec(yc), wspec(wa), wspec(wb), wspec(wc),
                  pl.BlockSpec((1, tn), lambda j, i: (0, j)), pl.BlockSpec((tm, tn), lambda j, i: (i, j))],
        out_specs=pl.BlockSpec((tm, tn), lambda j, i: (i, j)),
        out_shape=jax.ShapeDtypeStruct((m, D_MODEL), F32),
        compiler_params=_cparams(("parallel", "parallel"), 48), name="outproj",
    )(ya, ob, yc, wa, wb, wc, bias, h)


def _ln_router_body(y_ref, g_ref, b_ref, rw_ref, rb_ref, h_ref, hb_ref, lg_ref):
    h = _ln_rows(y_ref[...], g_ref[...], b_ref[...])
    h_ref[...] = h
    hb_ref[...] = h.astype(BF16)
    lg_ref[...] = jnp.dot(h, rw_ref[...], precision=HIGHEST, preferred_element_type=F32) + rb_ref[...]


def _ln_router(y, g, b, rw_pad, rb_pad, tm):
    m, d = y.shape
    row = pl.BlockSpec((tm, d), lambda i: (i, 0))
    vec = pl.BlockSpec((1, d), lambda i: (0, 0))
    return pl.pallas_call(
        _ln_router_body, grid=(m // tm,),
        in_specs=[row, vec, vec, pl.BlockSpec((d, LANES), lambda i: (0, 0)), pl.BlockSpec((1, LANES), lambda i: (0, 0))],
        out_specs=[row, row, pl.BlockSpec((tm, LANES), lambda i: (i, 0))],
        out_shape=[jax.ShapeDtypeStruct((m, d), F32), jax.ShapeDtypeStruct((m, d), BF16),
                   jax.ShapeDtypeStruct((m, LANES), F32)],
        compiler_params=_cparams(("parallel",), 48), name="ln_router",
    )(y, g, b, rw_pad, rb_pad)


def _weight_stream(w_hbms, stages, sem, be_ref, nb_ref, na_ref, cnt_ref, *, layer, tn, on_ready):
    j = pl.program_id(0)
    i = pl.program_id(1)
    n_j = pl.num_programs(0)
    na = na_ref[0]
    active = i < na
    fresh = active & ((i == 0) | (be_ref[i] != be_ref[jnp.maximum(i - 1, 0)]))

    def copies(e, jj, slot):
        cols = pl.ds(pl.multiple_of(jj * tn, tn), tn)
        return [pltpu.make_async_copy(w.at[layer, e, :, cols], st.at[slot], sem.at[t, slot])
                for t, (w, st) in enumerate(zip(w_hbms, stages))]

    @pl.when((j == 0) & (i == 0))
    def _():
        cnt_ref[0] = 0
        for cp in copies(be_ref[0], 0, 0):
            cp.start()

    @pl.when(fresh)
    def _():
        slot = cnt_ref[0] % 2
        for cp in copies(be_ref[i], j, slot):
            cp.wait()
        nxt = nb_ref[i]
        same_j = nxt < na
        e_next = be_ref[jnp.where(same_j, nxt, 0)]
        j_next = jnp.where(same_j, j, j + 1)

        @pl.when(same_j | (j + 1 < n_j))
        def _():
            for cp in copies(e_next, j_next, 1 - slot):
                cp.start()

        on_ready(slot)
        cnt_ref[0] = cnt_ref[0] + 1

    return active


def _moe_up_body(be_ref, nb_ref, na_ref, x_ref, bg_ref, bu_ref, wg_hbm, wu_hbm, a_ref,
                 stage_g, stage_u, wg_bf, wu_bf, sem, cnt_ref, *, layer, tn):
    def on_ready(slot):
        wg_bf[...] = stage_g[slot].astype(BF16)
        wu_bf[...] = stage_u[slot].astype(BF16)

    active = _weight_stream((wg_hbm, wu_hbm), (stage_g, stage_u), sem, be_ref, nb_ref, na_ref, cnt_ref,
                            layer=layer, tn=tn, on_ready=on_ready)

    @pl.when(active)
    def _():
        x = x_ref[...]
        g = jnp.dot(x, wg_bf[...], preferred_element_type=F32) + bg_ref[...]
        u = jnp.dot(x, wu_bf[...], preferred_element_type=F32) + bu_ref[...]
        g = jnp.minimum(g, SWIGLU_LIMIT)
        u = jnp.clip(u, -SWIGLU_LIMIT, SWIGLU_LIMIT)
        a_ref[...] = (g * jax.nn.sigmoid(SWIGLU_ALPHA * g) * (u + 1.0)).astype(BF16)

    @pl.when(jnp.logical_not(active))
    def _():
        a_ref[...] = jnp.zeros_like(a_ref)


def _moe_down_body(be_ref, nb_ref, na_ref, a_ref, bd_ref, wd_hbm, y_ref, stage_d, wd_bf, sem, cnt_ref, *, layer, tn):
    def on_ready(slot):
        wd_bf[...] = stage_d[slot].astype(BF16)

    active = _weight_stream((wd_hbm,), (stage_d,), sem, be_ref, nb_ref, na_ref, cnt_ref,
                            layer=layer, tn=tn, on_ready=on_ready)

    @pl.when(active)
    def _():
        y_ref[...] = jnp.dot(a_ref[...], wd_bf[...], preferred_element_type=F32) + bd_ref[...]

    @pl.when(jnp.logical_not(active))
    def _():
        y_ref[...] = jnp.zeros_like(y_ref)


def _moe_experts(layer, xs, block_expert, next_block, n_active, w_gate, b_gate, w_up, b_up, w_down, b_down):
    m_pad, d = xs.shape
    d_ff = w_gate.shape[-1]
    n_blk = m_pad // MOE_TM

    def blk(i, na):
        return jnp.minimum(i, na[0] - 1)

    def rows(width):
        return pl.BlockSpec((MOE_TM, width), lambda j, i, be, nb, na: (blk(i, na), 0))

    def bias(t):
        return pl.BlockSpec((None, None, 1, t), lambda j, i, be, nb, na: (layer, be[i], 0, j))

    def out(t):
        return pl.BlockSpec((MOE_TM, t), lambda j, i, be, nb, na: (i, j))

    hbm = pl.BlockSpec(memory_space=pl.ANY)
    smem_cnt = pltpu.SMEM((1,), jnp.int32)

    tn = MOE_TN_UP
    act = pl.pallas_call(
        functools.partial(_moe_up_body, layer=layer, tn=tn),
        grid_spec=pltpu.PrefetchScalarGridSpec(
            num_scalar_prefetch=3, grid=(d_ff // tn, n_blk),
            in_specs=[rows(d), bias(tn), bias(tn), hbm, hbm],
            out_specs=out(tn),
            scratch_shapes=[pltpu.VMEM((2, d, tn), F32), pltpu.VMEM((2, d, tn), F32),
                            pltpu.VMEM((d, tn), BF16), pltpu.VMEM((d, tn), BF16),
                            pltpu.SemaphoreType.DMA((2, 2)), smem_cnt]),
        out_shape=jax.ShapeDtypeStruct((m_pad, d_ff), BF16),
        compiler_params=_cparams(("arbitrary", "arbitrary"), 56), name="moe_gate_up",
    )(block_expert, next_block, n_active, xs,
      b_gate.reshape(DEPTH, N_EXPERTS, 1, d_ff), b_up.reshape(DEPTH, N_EXPERTS, 1, d_ff), w_gate, w_up)

    tn2 = MOE_TN_DOWN
    return pl.pallas_call(
        functools.partial(_moe_down_body, layer=layer, tn=tn2),
        grid_spec=pltpu.PrefetchScalarGridSpec(
            num_scalar_prefetch=3, grid=(d // tn2, n_blk),
            in_specs=[rows(d_ff), bias(tn2), hbm],
            out_specs=out(tn2),
            scratch_shapes=[pltpu.VMEM((2, d_ff, tn2), F32), pltpu.VMEM((d_ff, tn2), BF16),
                            pltpu.SemaphoreType.DMA((1, 2)), smem_cnt]),
        out_shape=jax.ShapeDtypeStruct((m_pad, d), F32),
        compiler_params=_cparams(("arbitrary", "arbitrary"), 56), name="moe_down",
    )(block_expert, next_block, n_active, act, b_down.reshape(DEPTH, N_EXPERTS, 1, d), w_down)


def _combine_body(dest_ref, h_ref, gt_ref, g_ref, b_ref, yb_hbm, o_ref, ob_ref, yg, sem, *, tm):
    t = pl.program_id(0)
    n_tiles = pl.num_programs(0)

    def row_copy(tile, slot, r, kk):
        d = dest_ref[(tile * tm + r) * TOP_K + kk]
        return pltpu.make_async_copy(yb_hbm.at[pl.ds(d, 1), :], yg.at[slot, kk, pl.ds(r, 1), :], sem.at[slot])

    def issue(tile, slot):
        def body(r, carry):
            for kk in range(TOP_K):
                row_copy(tile, slot, r, kk).start()
            return carry
        lax.fori_loop(0, tm, body, 0)

    @pl.when(t == 0)
    def _():
        issue(0, 0)

    slot = t % 2

    @pl.when(t + 1 < n_tiles)
    def _():
        issue(t + 1, 1 - slot)

    def wait_body(r, carry):
        for kk in range(TOP_K):
            row_copy(t, slot, r, kk).wait()
        return carry
    lax.fori_loop(0, tm, wait_body, 0)

    gt = gt_ref[...]
    moe = ((yg[slot, 0] * gt[:, 0:1] + yg[slot, 1] * gt[:, 1:2])
           + (yg[slot, 2] * gt[:, 2:3] + yg[slot, 3] * gt[:, 3:4]))
    y = _ln_rows(DN_ALPHA * h_ref[...] + moe, g_ref[...], b_ref[...])
    o_ref[...] = y
    ob_ref[...] = y.astype(BF16)


def _combine_ln(h1, yb, dest, gates, g, b, tm):
    m, d = h1.shape
    row = pl.BlockSpec((tm, d), lambda i, dr: (i, 0))
    vec = pl.BlockSpec((1, d), lambda i, dr: (0, 0))
    return pl.pallas_call(
        functools.partial(_combine_body, tm=tm),
        grid_spec=pltpu.PrefetchScalarGridSpec(
            num_scalar_prefetch=1, grid=(m // tm,),
            in_specs=[row, pl.BlockSpec((tm, TOP_K), lambda i, dr: (i, 0)), vec, vec,
                      pl.BlockSpec(memory_space=pl.ANY)],
            out_specs=[row, row],
            scratch_shapes=[pltpu.VMEM((2, TOP_K, tm, d), F32), pltpu.SemaphoreType.DMA((2,))]),
        out_shape=[jax.ShapeDtypeStruct((m, d), F32), jax.ShapeDtypeStruct((m, d), BF16)],
        compiler_params=_cparams(("arbitrary",), 48), name="combine_ln2",
    )(dest, h1, gates, g, b, yb)


def _route(logits):
    n_tok = logits.shape[0]
    top_v, top_e = lax.top_k(logits, TOP_K)
    gates = jax.nn.softmax(top_v, axis=-1)
    m = n_tok * TOP_K
    e_flat = top_e.reshape(m)
    n_blk = -(-m // MOE_TM) + N_EXPERTS
    onehot = jax.nn.one_hot(e_flat, N_EXPERTS, dtype=jnp.int32)
    counts = jnp.sum(onehot, axis=0)
    first = jnp.cumsum(counts) - counts
    order = jnp.argsort(e_flat).astype(jnp.int32)
    pos = jnp.argsort(order).astype(jnp.int32)
    blocks_per = (counts + MOE_TM - 1) // MOE_TM
    blocks_end = jnp.cumsum(blocks_per)
    row0 = (blocks_end - blocks_per) * MOE_TM
    dest = pos + jnp.sum(onehot * (row0 - first)[None, :], axis=1)
    n_active = blocks_end[-1]
    bidx = jnp.minimum(jnp.arange(n_blk), n_active - 1)
    block_expert = jnp.minimum(jnp.sum(bidx[:, None] >= blocks_end[None, :], axis=1), N_EXPERTS - 1).astype(jnp.int32)
    next_block = blocks_end[block_expert].astype(jnp.int32)
    off = (jnp.arange(n_blk)[:, None] * MOE_TM - row0[block_expert][:, None]) + jnp.arange(MOE_TM)[None, :]
    valid = (off < counts[block_expert][:, None]) & (jnp.arange(n_blk)[:, None] < n_active)
    sorted_pos = jnp.clip(first[block_expert][:, None] + off, 0, m - 1)
    src = jnp.where(valid, order[sorted_pos] // TOP_K, 0).reshape(n_blk * MOE_TM)
    return gates, dest.astype(jnp.int32), src, block_expert, next_block, n_active.reshape(1).astype(jnp.int32)


def _rope_tables(pos):
    half = HEAD_DIM // 2
    inv_freq = ROPE_THETA ** (-np.arange(half, dtype=np.float64) / half)
    ang = np.asarray(pos, np.float64)[:, None] * inv_freq[None, :]
    cos, sin = np.cos(ang), np.sin(ang)
    return (jnp.asarray(np.concatenate([cos, cos], axis=1), F32),
            jnp.asarray(np.concatenate([-sin, sin], axis=1), F32))


def _split_w_in(w, b):
    k = w.shape[0]
    w = w.astype(BF16)
    w_r = jnp.concatenate([w[:, _O_QB:_O_VB], w[:, _O_QC:_O_VC], jnp.zeros((k, LANES), w.dtype)], axis=1)
    w_p = jnp.concatenate([w[:, _O_QA:_O_LR], w[:, _O_VB:_O_QC], w[:, _O_VC:_O_END], w[:, _O_LR:_O_QB],
                           jnp.zeros((k, LANES - A_LOWRANK), w.dtype)], axis=1)
    b_r = jnp.concatenate([b[_O_QB:_O_VB], b[_O_QC:_O_VC], jnp.zeros((LANES,), b.dtype)])
    b_p = jnp.concatenate([b[_O_QA:_O_LR], b[_O_VB:_O_QC], b[_O_VC:_O_END], b[_O_LR:_O_QB],
                           jnp.zeros((LANES - A_LOWRANK,), b.dtype)])
    return w_r, w_p, b_r.reshape(1, W_PROJ), b_p.reshape(1, W_PROJ)


def _mix_prompt(rp, pp, wlr_pad, blr, ng, sink, bp, tlen):
    ya, st = _gla_prompt(pp, wlr_pad, blr, ng, bp, tlen)
    ob = _swa_attention(rp, pp, sink, bp, tlen)
    yc = _dilated_attention(rp, pp, bp, tlen)
    return ya, ob, yc, st


def kernel(x_prompt, x_sample, cache_b_kv, cache_c1_kv, cache_c2_kv, cache_c3_kv, state_gla, ln_in_g, ln_in_b, w_in, b_in, gla_w_gate, gla_b_gate, gla_norm_g, attn_sinks, w_out, b_out, ln1_g, ln1_b, router_w, router_b, w_gate, b_gate, w_up, b_up, w_down, b_down, ln2_g, ln2_b):
    bp, tlen, d = x_prompt.shape
    bs, slen, _ = x_sample.shape
    assert slen == 1 and d == D_MODEL and tlen % SWA_ROWS == 0
    n_p, n_s = bp * tlen, bs * slen
    tm_p = 256

    cos_p, sin_p = _rope_tables(np.tile(np.arange(tlen), bp))
    cos_s, sin_s = _rope_tables(np.full((n_s,), PAST_LEN))
    caches_c = (cache_c1_kv, cache_c2_kv, cache_c3_kv)

    hp, hp_b = _ln_in(x_prompt.reshape(n_p, d), ln_in_g, ln_in_b, tm_p)
    hs, hs_b = _ln_in(x_sample.reshape(n_s, d), ln_in_g, ln_in_b, n_s)

    new_p = [[] for _ in range(5)]
    new_s = [[] for _ in range(5)]
    for l in range(DEPTH):
        w_r, w_p, b_r, b_p = _split_w_in(w_in[l], b_in[l])
        wlr_pad = jnp.concatenate([gla_w_gate[l], jnp.zeros((LANES - A_LOWRANK, A_HEADS * A_DK), F32)], axis=0)
        blr = gla_b_gate[l].reshape(1, A_HEADS * A_DK)
        ng = gla_norm_g[l].reshape(1, A_DV)
        sink = attn_sinks[l].reshape(B_HEADS)
        wo = w_out[l].astype(BF16)
        na, nb = A_HEADS * A_DV, B_HEADS * HEAD_DIM
        wo_a, wo_b, wo_c = wo[:na], wo[na:na + nb], wo[na + nb:]
        bo = b_out[l].reshape(1, d)

        rp = _proj(hp_b, w_r, b_r, cos_p, sin_p, tm=512)
        pp = _proj(hp_b, w_p, b_p, tm=512)
        rs = _proj(hs_b, w_r, b_r, cos_s, sin_s, tm=n_s)
        ps = _proj(hs_b, w_p, b_p, tm=n_s)

        ya_p, ob_p, yc_p, st_p = _mix_prompt(rp, pp, wlr_pad, blr, ng, sink, bp, tlen)
        ya_s, ob_s, yc_s, st_s = _sample_mixers(l, rs, ps, state_gla, cache_b_kv, caches_c, sink, wlr_pad, blr, ng)

        rp4 = rp.reshape(bp, tlen, W_PROJ)
        pp4 = pp.reshape(bp, tlen, W_PROJ)

        def kv_prompt(uk, uv, heads, rows):
            kk = rp4[:, tlen - rows:, uk * LANES:(uk + heads) * LANES].reshape(bp, rows, heads, HEAD_DIM)
            vv = pp4[:, tlen - rows:, uv * LANES:(uv + heads) * LANES].reshape(bp, rows, heads, HEAD_DIM)
            return jnp.stack([kk, vv], axis=1)

        def kv_sample(uk, uv, heads):
            kk = rs[:, uk * LANES:(uk + heads) * LANES].reshape(bs, slen, heads, HEAD_DIM)
            vv = ps[:, uv * LANES:(uv + heads) * LANES].reshape(bs, slen, heads, HEAD_DIM)
            return jnp.stack([kk, vv], axis=1)

        new_p[0].append(kv_prompt(R_KB, P_VB, B_KV_HEADS, min(B_WINDOW, tlen)))
        new_s[0].append(kv_sample(R_KB, P_VB, B_KV_HEADS))
        for g, (win, dil) in enumerate(C_PATTERNS):
            new_p[1 + g].append(kv_prompt(R_KC + C_HEADS * g, P_VC + C_HEADS * g, C_HEADS, min(win, tlen)))
            new_s[1 + g].append(kv_sample(R_KC + C_HEADS * g, P_VC + C_HEADS * g, C_HEADS))
        new_p[4].append(st_p)
        new_s[4].append(st_s)

        y_p = _outproj(ya_p, ob_p, yc_p, wo_a, wo_b, wo_c, bo, hp, tm=512)
        y_s = _outproj(ya_s, ob_s, yc_s, wo_a, wo_b, wo_c, bo, hs, tm=n_s)
        rw_pad = jnp.concatenate([router_w[l], jnp.zeros((d, LANES - N_EXPERTS), F32)], axis=1)
        rb_pad = jnp.concatenate([router_b[l], jnp.zeros((LANES - N_EXPERTS,), F32)]).reshape(1, LANES)
        g1, b1 = ln1_g[l].reshape(1, d), ln1_b[l].reshape(1, d)
        h1_p, h1b_p, lg_p = _ln_router(y_p, g1, b1, rw_pad, rb_pad, tm_p)
        h1_s, h1b_s, lg_s = _ln_router(y_s, g1, b1, rw_pad, rb_pad, n_s)

        logits = jnp.concatenate([lg_p, lg_s], axis=0)[:, :N_EXPERTS]
        gates, dest, src, block_expert, next_block, n_active = _route(logits)
        h1b = jnp.concatenate([h1b_p, h1b_s], axis=0)
        xs = h1b[src]
        yb = _moe_experts(l, xs, block_expert, next_block, n_active, w_gate, b_gate, w_up, b_up, w_down, b_down)
        g2, b2 = ln2_g[l].reshape(1, d), ln2_b[l].reshape(1, d)
        hp, hp_b = _combine_ln(h1_p, yb, dest[:n_p * TOP_K], gates[:n_p], g2, b2, 128)
        hs, hs_b = _combine_ln(h1_s, yb, dest[n_p * TOP_K:], gates[n_p:], g2, b2, n_s)

    outs = [hp.reshape(bp, tlen, d), hs.reshape(bs, slen, d)]
    for i in range(5):
        outs += [jnp.stack(new_p[i], axis=1), jnp.stack(new_s[i], axis=1)]
    return tuple(outs)
```

```python
import functools

import numpy as np
import jax
import jax.numpy as jnp
from jax import lax
from jax.experimental import pallas as pl
from jax.experimental.pallas import tpu as pltpu

F32 = jnp.float32
BF16 = jnp.bfloat16
HIGHEST = lax.Precision.HIGHEST

LANES = 128
SUBLANES = 8

D_MODEL = 4096
DEPTH = 2
PAST_LEN = 16384
HEAD_DIM = 128
A_HEADS, A_DK, A_DV, A_LOWRANK = 4, 128, 256, 16
A_GATE_NORM = 16.0
A_CHUNK = 64
B_HEADS, B_KV_HEADS = 12, 3
B_GROUP = B_HEADS // B_KV_HEADS
B_WINDOW = 128
C_PATTERNS = ((128, 1), (512, 4), (2048, 16))
N_C_GROUPS = len(C_PATTERNS)
C_HEADS = 4
ATTN_BLOCK = 128
ROPE_THETA = 10000.0
N_EXPERTS = 32
TOP_K = 4
SWIGLU_LIMIT = 7.0
SWIGLU_ALPHA = 1.702
DN_ALPHA = (2 * DEPTH) ** 0.25
LN_EPS = 1e-5
RMS_EPS = 1e-6
NEG_INF = -1e30
ATTN_SCALE = HEAD_DIM ** -0.5

_O_QA, _O_KA, _O_VA, _O_GA, _O_LR, _O_QB, _O_KB, _O_VB, _O_QC, _O_KC, _O_VC, _O_END = (
    0, 512, 1024, 2048, 3072, 3088, 4624, 5008, 5392, 6928, 8464, 10000)
R_QB, R_KB, R_QC, R_KC = 0, 12, 15, 27
P_QA, P_KA, P_VA, P_GA, P_VB, P_VC, P_LR = 0, 4, 8, 16, 24, 27, 39
U_PROJ = 40
W_PROJ = U_PROJ * LANES

S_ROWS = 16 + SUBLANES * N_C_GROUPS

MOE_TM = 256
MOE_TN_UP = 512
MOE_TN_DOWN = 1024
SWA_ROWS = 512


def _cparams(sem, vmem_mb):
    return pltpu.CompilerParams(dimension_semantics=sem, vmem_limit_bytes=vmem_mb * 1024 * 1024)


def _ln_rows(x, g, b):
    mu = jnp.mean(x, axis=-1, keepdims=True)
    xc = x - mu
    var = jnp.mean(xc * xc, axis=-1, keepdims=True)
    return xc * lax.rsqrt(var + LN_EPS) * g + b


def _ln_in_body(x_ref, g_ref, b_ref, o_ref, ob_ref):
    y = _ln_rows(x_ref[...], g_ref[...], b_ref[...])
    o_ref[...] = y
    ob_ref[...] = y.astype(BF16)


def _ln_in(x, g, b, tm):
    m, d = x.shape
    row = pl.BlockSpec((tm, d), lambda i: (i, 0))
    vec = pl.BlockSpec((1, d), lambda i: (0, 0))
    return pl.pallas_call(
        _ln_in_body, grid=(m // tm,), in_specs=[row, vec, vec], out_specs=[row, row],
        out_shape=[jax.ShapeDtypeStruct((m, d), F32), jax.ShapeDtypeStruct((m, d), BF16)],
        compiler_params=_cparams(("parallel",), 48), name="ln_in",
    )(x, g.reshape(1, d), b.reshape(1, d))


def _proj_body(*refs, rope, tn):
    if rope:
        x_ref, w_ref, b_ref, cos_ref, sin_ref, o_ref = refs
    else:
        x_ref, w_ref, b_ref, o_ref = refs
    acc = jnp.dot(x_ref[...], w_ref[...], preferred_element_type=F32) + b_ref[...]
    if not rope:
        o_ref[...] = acc
        return
    cos = cos_ref[...]
    sin = sin_ref[...]
    for u in range(tn // LANES):
        seg = acc[:, u * LANES:(u + 1) * LANES]
        o_ref[:, u * LANES:(u + 1) * LANES] = seg * cos + pltpu.roll(seg, HEAD_DIM // 2, 1) * sin


def _proj(xb, w, b, cos=None, sin=None, *, tm, tn=1024):
    m, k = xb.shape
    n = w.shape[1]
    rope = cos is not None
    in_specs = [pl.BlockSpec((tm, k), lambda j, i: (i, 0)),
                pl.BlockSpec((k, tn), lambda j, i: (0, j)),
                pl.BlockSpec((1, tn), lambda j, i: (0, j))]
    args = [xb, w, b]
    if rope:
        tab = pl.BlockSpec((tm, LANES), lambda j, i: (i, 0))
        in_specs += [tab, tab]
        args += [cos, sin]
    return pl.pallas_call(
        functools.partial(_proj_body, rope=rope, tn=tn), grid=(n // tn, m // tm),
        in_specs=in_specs, out_specs=pl.BlockSpec((tm, tn), lambda j, i: (i, j)),
        out_shape=jax.ShapeDtypeStruct((m, n), F32),
        compiler_params=_cparams(("parallel", "parallel"), 48),
        name="proj_rope" if rope else "proj_plain",
    )(*args)


def _log_sigmoid(z):
    return jnp.minimum(z, 0.0) - jnp.log(1.0 + jnp.exp(-jnp.abs(z)))


def _rms_gate(o, ng, ga):
    o = o * lax.rsqrt(jnp.mean(o * o, axis=-1, keepdims=True) + RMS_EPS) * ng
    return o * (ga * jax.nn.sigmoid(ga))


def _gla_body(q_ref, k_ref, v_ref, ga_ref, lr_ref, wlr_ref, blr_ref, ng_ref, ya_ref, st_ref, s_scr, *, n_sub):
    c = pl.program_id(2)

    @pl.when(c == 0)
    def _():
        s_scr[...] = jnp.zeros_like(s_scr)

    ch = A_CHUNK
    rr = lax.broadcasted_iota(jnp.int32, (ch, ch), 0)
    cc = lax.broadcasted_iota(jnp.int32, (ch, ch), 1)
    causal = rr >= cc
    tri = causal.astype(F32)
    ones = jnp.ones((ch, A_DK), F32)
    for u in range(n_sub):
        sl = pl.ds(u * ch, ch)
        z = jnp.dot(lr_ref[sl, :], wlr_ref[...], precision=HIGHEST, preferred_element_type=F32) + blr_ref[...]
        g = _log_sigmoid(z) / A_GATE_NORM
        b = jnp.dot(tri, g, precision=HIGHEST, preferred_element_type=F32)
        b_last = b[ch - 1:ch, :]
        b_last_col = lax.dot_general(g, ones, (((0,), (0,)), ((), ())), precision=HIGHEST,
                                     preferred_element_type=F32)
        q = q_ref[sl, :] * (A_DK ** -0.5)
        k = k_ref[sl, :]
        v = v_ref[sl, :].astype(BF16)
        qg = (q * jnp.exp(b)).astype(BF16)
        kg = (k * jnp.exp(-b)).astype(BF16)
        kd = (k * jnp.exp(b_last - b)).astype(BF16)
        att = lax.dot_general(qg, kg, (((1,), (1,)), ((), ())), preferred_element_type=F32)
        att = jnp.where(causal, att, 0.0).astype(BF16)
        s = s_scr[...]
        o = (jnp.dot(qg, s.astype(BF16), preferred_element_type=F32)
             + jnp.dot(att, v, preferred_element_type=F32))
        decay = jnp.exp(b_last_col)
        decay = jnp.concatenate([decay, decay], axis=1)
        s_scr[...] = decay * s + lax.dot_general(kd, v, (((0,), (0,)), ((), ())), preferred_element_type=F32)
        ya_ref[sl, :] = _rms_gate(o, ng_ref[...], ga_ref[sl, :])

    @pl.when(c == pl.num_programs(2) - 1)
    def _():
        st_ref[...] = s_scr[...]


def _gla_prompt(p_arr, wlr_pad, blr, ng, bsz, tlen, rb=256):
    nblk = tlen // rb
    n_sub = rb // A_CHUNK

    def rows(b, c):
        return b * nblk + c

    in_specs = [
        pl.BlockSpec((rb, A_DK), lambda b, h, c: (rows(b, c), P_QA + h)),
        pl.BlockSpec((rb, A_DK), lambda b, h, c: (rows(b, c), P_KA + h)),
        pl.BlockSpec((rb, A_DV), lambda b, h, c: (rows(b, c), P_VA // 2 + h)),
        pl.BlockSpec((rb, A_DV), lambda b, h, c: (rows(b, c), P_GA // 2 + h)),
        pl.BlockSpec((rb, LANES), lambda b, h, c: (rows(b, c), P_LR)),
        pl.BlockSpec((LANES, A_DK), lambda b, h, c: (0, h)),
        pl.BlockSpec((1, A_DK), lambda b, h, c: (0, h)),
        pl.BlockSpec((1, A_DV), lambda b, h, c: (0, 0)),
    ]
    out_specs = [
        pl.BlockSpec((rb, A_DV), lambda b, h, c: (rows(b, c), h)),
        pl.BlockSpec((None, None, A_DK, A_DV), lambda b, h, c: (b, h, 0, 0)),
    ]
    return pl.pallas_call(
        functools.partial(_gla_body, n_sub=n_sub), grid=(bsz, A_HEADS, nblk),
        in_specs=in_specs, out_specs=out_specs,
        out_shape=[jax.ShapeDtypeStruct((bsz * tlen, A_HEADS * A_DV), F32),
                   jax.ShapeDtypeStruct((bsz, A_HEADS, A_DK, A_DV), F32)],
        scratch_shapes=[pltpu.VMEM((A_DK, A_DV), F32)],
        compiler_params=_cparams(("parallel", "parallel", "arbitrary"), 32), name="gla_prompt",
    )(p_arr, p_arr, p_arr, p_arr, p_arr, wlr_pad, blr, ng)


def _attn_block(q16, k_prev, k_cur, v_prev, v_cur, prev_on, sink):
    blk = ATTN_BLOCK
    rows = q16.shape[0]
    nk = blk if k_prev is None else 2 * blk
    r = lax.broadcasted_iota(jnp.int32, (rows, nk), 0)
    if rows > blk:
        r = r % blk
    c = lax.broadcasted_iota(jnp.int32, (rows, nk), 1)
    if k_prev is None:
        keys, vals = k_cur.astype(BF16), v_cur.astype(BF16)
        mask = c <= r
    else:
        keys = jnp.concatenate([k_prev, k_cur], axis=0).astype(BF16)
        vals = jnp.concatenate([v_prev, v_cur], axis=0).astype(BF16)
        in_prev = (c < blk) & (c >= r)
        if prev_on is not True:
            in_prev = in_prev & prev_on
        mask = in_prev | ((c >= blk) & (c - blk <= r))
    s = lax.dot_general(q16, keys, (((1,), (1,)), ((), ())), preferred_element_type=F32) * ATTN_SCALE
    s = jnp.where(mask, s, NEG_INF)
    m = jnp.max(s, axis=1, keepdims=True)
    if sink is not None:
        m = jnp.maximum(m, sink)
    p = jnp.exp(s - m)
    den = jnp.sum(p, axis=1, keepdims=True)
    if sink is not None:
        den = den + jnp.exp(sink - m)
    o = jnp.dot(p.astype(BF16), vals, preferred_element_type=F32) / den
    return o, m + jnp.log(den)


def _swa_body(sink_ref, q_ref, kc_ref, vc_ref, kp_ref, vp_ref, o_ref):
    blk = ATTN_BLOCK
    hk = pl.program_id(1)
    n = pl.program_id(2)
    sink = jnp.concatenate([jnp.full((blk, 1), sink_ref[hk * B_GROUP + g], F32) for g in range(B_GROUP)], axis=0)
    for i in range(SWA_ROWS // blk):
        rows = slice(i * blk, (i + 1) * blk)
        q = q_ref[rows, :]
        q16 = jnp.concatenate([q[:, g * HEAD_DIM:(g + 1) * HEAD_DIM] for g in range(B_GROUP)], axis=0).astype(BF16)
        if i == 0:
            k_prev, v_prev, prev_on = kp_ref[...], vp_ref[...], n > 0
        else:
            prev = slice((i - 1) * blk, i * blk)
            k_prev, v_prev, prev_on = kc_ref[prev, :], vc_ref[prev, :], True
        o, _ = _attn_block(q16, k_prev, kc_ref[rows, :], v_prev, vc_ref[rows, :], prev_on, sink)
        o_ref[rows, :] = jnp.concatenate([o[g * blk:(g + 1) * blk, :] for g in range(B_GROUP)], axis=1)


def _swa_attention(r_arr, p_arr, sink, bsz, tlen):
    nsb = tlen // SWA_ROWS
    sub = SWA_ROWS // ATTN_BLOCK
    gw = B_GROUP * HEAD_DIM

    def prev_blk(b, n):
        return (b * nsb + n) * sub - jnp.minimum(n, 1)

    in_specs = [
        pl.BlockSpec(memory_space=pltpu.SMEM),
        pl.BlockSpec((SWA_ROWS, gw), lambda b, h, n: (b * nsb + n, R_QB // B_GROUP + h)),
        pl.BlockSpec((SWA_ROWS, HEAD_DIM), lambda b, h, n: (b * nsb + n, R_KB + h)),
        pl.BlockSpec((SWA_ROWS, HEAD_DIM), lambda b, h, n: (b * nsb + n, P_VB + h)),
        pl.BlockSpec((ATTN_BLOCK, HEAD_DIM), lambda b, h, n: (prev_blk(b, n), R_KB + h)),
        pl.BlockSpec((ATTN_BLOCK, HEAD_DIM), lambda b, h, n: (prev_blk(b, n), P_VB + h)),
    ]
    return pl.pallas_call(
        _swa_body, grid=(bsz, B_KV_HEADS, nsb), in_specs=in_specs,
        out_specs=pl.BlockSpec((SWA_ROWS, gw), lambda b, h, n: (b * nsb + n, h)),
        out_shape=jax.ShapeDtypeStruct((bsz * tlen, B_HEADS * HEAD_DIM), F32),
        compiler_params=_cparams(("parallel", "parallel", "parallel"), 32), name="swa_attn",
    )(sink, r_arr, r_arr, p_arr, r_arr, p_arr)


def _dilated_body(*refs, tlen):
    qkv = refs[:3 * N_C_GROUPS]
    yc_ref = refs[3 * N_C_GROUPS]
    o_scr = refs[3 * N_C_GROUPS + 1:3 * N_C_GROUPS + 1 + N_C_GROUPS]
    l_scr = refs[3 * N_C_GROUPS + 1 + N_C_GROUPS:]
    blk = ATTN_BLOCK
    for g, (win, dil) in enumerate(C_PATTERNS):
        q_ref, k_ref, v_ref = qkv[3 * g:3 * g + 3]
        for res in range(dil):
            for i in range(tlen // dil // blk):
                def rows(ii):
                    start = res + dil * blk * ii
                    return pl.ds(start, blk) if dil == 1 else pl.ds(start, blk, stride=dil)
                cur = rows(i)
                if i == 0:
                    k_prev = v_prev = None
                else:
                    k_prev, v_prev = k_ref[rows(i - 1), :], v_ref[rows(i - 1), :]
                o, lse = _attn_block(q_ref[cur, :].astype(BF16), k_prev, k_ref[cur, :], v_prev, v_ref[cur, :], True, None)
                o_scr[g][cur, :] = o
                l_scr[g][cur, :] = jnp.broadcast_to(lse, (blk, HEAD_DIM))
    step = 2 * blk
    for ch in range(tlen // step):
        rows = slice(ch * step, (ch + 1) * step)
        ls = [l[rows, :] for l in l_scr]
        lmax = jnp.maximum(jnp.maximum(ls[0], ls[1]), ls[2])
        es = [jnp.exp(l - lmax) for l in ls]
        num = es[0] * o_scr[0][rows, :] + es[1] * o_scr[1][rows, :] + es[2] * o_scr[2][rows, :]
        yc_ref[rows, :] = num / (es[0] + es[1] + es[2])


def _dilated_attention(r_arr, p_arr, bsz, tlen):
    in_specs, args = [], []
    for g, (win, dil) in enumerate(C_PATTERNS):
        assert win // dil == ATTN_BLOCK and tlen % (dil * ATTN_BLOCK) == 0
        in_specs += [pl.BlockSpec((tlen, HEAD_DIM), lambda b, h, g=g: (b, R_QC + C_HEADS * g + h)),
                     pl.BlockSpec((tlen, HEAD_DIM), lambda b, h, g=g: (b, R_KC + C_HEADS * g + h)),
                     pl.BlockSpec((tlen, HEAD_DIM), lambda b, h, g=g: (b, P_VC + C_HEADS * g + h))]
        args += [r_arr, r_arr, p_arr]
    return pl.pallas_call(
        functools.partial(_dilated_body, tlen=tlen), grid=(bsz, C_HEADS), in_specs=in_specs,
        out_specs=pl.BlockSpec((tlen, HEAD_DIM), lambda b, h: (b, h)),
        out_shape=jax.ShapeDtypeStruct((bsz * tlen, C_HEADS * HEAD_DIM), F32),
        scratch_shapes=[pltpu.VMEM((tlen, HEAD_DIM), F32) for _ in range(2 * N_C_GROUPS)],
        compiler_params=_cparams(("parallel", "parallel"), 40), name="dilated_attn",
    )(*args)


def _sample_body(sink_ref, q_ref, kn_ref, vn_ref, qk_ref, va_ref, ga_ref, lr_ref, st_ref,
                 cb_ref, c1_ref, c2_ref, c3_ref, wlr_ref, blr_ref, ng_ref,
                 ya_ref, ob_ref, yc_ref, sn_ref):
    nt = (((1,), (1,)), ((), ()))

    z = jnp.dot(lr_ref[...], wlr_ref[...], precision=HIGHEST, preferred_element_type=F32)[0:1, :] + blr_ref[...]
    a_all = jnp.exp(_log_sigmoid(z) / A_GATE_NORM)
    qk = qk_ref[...]
    rid16 = lax.broadcasted_iota(jnp.int32, (16, 1), 0)
    rows = jnp.zeros((16, A_DK), F32)
    for h in range(A_HEADS):
        a_h = jnp.broadcast_to(a_all[:, h * A_DK:(h + 1) * A_DK], (16, A_DK))
        k_h = jnp.broadcast_to(qk[A_HEADS + h:A_HEADS + h + 1, :], (16, A_DK))
        q_h = jnp.broadcast_to(qk[h:h + 1, :], (16, A_DK)) * (A_DK ** -0.5)
        rows = jnp.where(rid16 == 3 * h, a_h, rows)
        rows = jnp.where(rid16 == 3 * h + 1, k_h, rows)
        rows = jnp.where(rid16 == 3 * h + 2, q_h, rows)
    cols = jnp.concatenate([rows, jnp.zeros((LANES - 16, A_DK), F32)], axis=0).T
    for h in range(A_HEADS):
        a_col, k_col, q_col = cols[:, 3 * h:3 * h + 1], cols[:, 3 * h + 1:3 * h + 2], cols[:, 3 * h + 2:3 * h + 3]
        s_new = a_col * st_ref[h] + k_col * va_ref[h:h + 1, :]
        sn_ref[h] = s_new
        o = jnp.sum(q_col * s_new, axis=0, keepdims=True)
        ya_ref[h:h + 1, :] = _rms_gate(o, ng_ref[...], ga_ref[h:h + 1, :])

    q = q_ref[...]
    q16 = q.astype(BF16)
    rid = lax.broadcasted_iota(jnp.int32, (S_ROWS, 1), 0)

    def pick_b(j):
        return (rid < B_HEADS) & (rid // B_GROUP == j)

    def pick_c(g, hh):
        return rid == 16 + SUBLANES * g + hh

    c_refs = (c1_ref, c2_ref, c3_ref)
    s = jnp.zeros((S_ROWS, ATTN_BLOCK), F32)
    for j in range(B_KV_HEADS):
        kj = cb_ref[0, :, j, :].astype(BF16)
        s = jnp.where(pick_b(j), lax.dot_general(q16, kj, nt, preferred_element_type=F32), s)
    for g in range(N_C_GROUPS):
        for hh in range(C_HEADS):
            kh = c_refs[g][0, :, hh, :].astype(BF16)
            s = jnp.where(pick_c(g, hh), lax.dot_general(q16, kh, nt, preferred_element_type=F32), s)
    s = s * ATTN_SCALE
    s_n = jnp.sum(q * kn_ref[...], axis=1, keepdims=True) * ATTN_SCALE
    sink = jnp.full((S_ROWS, 1), NEG_INF, F32)
    for i in range(B_HEADS):
        sink = jnp.where(rid == i, sink_ref[i], sink)
    m = jnp.maximum(jnp.maximum(jnp.max(s, axis=1, keepdims=True), s_n), sink)
    p = jnp.exp(s - m)
    p_n = jnp.exp(s_n - m)
    den = jnp.sum(p, axis=1, keepdims=True) + p_n + jnp.exp(sink - m)
    p16 = p.astype(BF16)
    acc = p_n * vn_ref[...]
    for j in range(B_KV_HEADS):
        vj = cb_ref[1, :, j, :].astype(BF16)
        acc = acc + jnp.where(pick_b(j), jnp.dot(p16, vj, preferred_element_type=F32), 0.0)
    for g in range(N_C_GROUPS):
        for hh in range(C_HEADS):
            vh = c_refs[g][1, :, hh, :].astype(BF16)
            acc = acc + jnp.where(pick_c(g, hh), jnp.dot(p16, vh, preferred_element_type=F32), 0.0)
    o = acc / den
    lse = m + jnp.log(den)
    ob_ref[...] = o[0:16, :]
    ls = [lse[16 + SUBLANES * g:24 + SUBLANES * g, :] for g in range(N_C_GROUPS)]
    os_ = [o[16 + SUBLANES * g:24 + SUBLANES * g, :] for g in range(N_C_GROUPS)]
    lmax = jnp.maximum(jnp.maximum(ls[0], ls[1]), ls[2])
    es = [jnp.exp(l - lmax) for l in ls]
    yc_ref[...] = (es[0] * os_[0] + es[1] * os_[1] + es[2] * os_[2]) / (es[0] + es[1] + es[2])


def _sample_mixers(layer, r_s, p_s, state_gla, cache_b, caches_c, sink, wlr_pad, blr, ng):
    bs = r_s.shape[0]
    r3 = r_s.reshape(bs, U_PROJ, LANES)
    p3 = p_s.reshape(bs, U_PROJ, LANES)
    z4 = jnp.zeros((bs, 4, LANES), F32)

    def qrows(x3, ub, uc, rep):
        parts = [jnp.repeat(x3[:, ub:ub + B_KV_HEADS], B_GROUP, axis=1) if rep else x3[:, ub:ub + B_HEADS], z4]
        for g in range(N_C_GROUPS):
            parts += [x3[:, uc + C_HEADS * g:uc + C_HEADS * (g + 1)], z4]
        return jnp.concatenate(parts, axis=1)

    q_rows = qrows(r3, R_QB, R_QC, False)
    kn_rows = qrows(r3, R_KB, R_KC, True)
    vn_rows = qrows(p3, P_VB, P_VC, True)
    qk_a = p3[:, P_QA:P_VA]
    v_a = p_s[:, P_VA * LANES:P_GA * LANES].reshape(bs, A_HEADS, A_DV)
    g_a = p_s[:, P_GA * LANES:P_VB * LANES].reshape(bs, A_HEADS, A_DV)
    lr = jnp.broadcast_to(p3[:, P_LR:P_LR + 1], (bs, SUBLANES, LANES))

    cc = [c.reshape(bs, DEPTH, 2, ATTN_BLOCK, dil, C_HEADS, HEAD_DIM) for c, (_, dil) in zip(caches_c, C_PATTERNS)]

    def tok(rows, width):
        return pl.BlockSpec((None, rows, width), lambda b: (b, 0, 0))

    in_specs = [
        pl.BlockSpec(memory_space=pltpu.SMEM),
        tok(S_ROWS, LANES), tok(S_ROWS, LANES), tok(S_ROWS, LANES),
        tok(2 * A_HEADS, LANES), tok(A_HEADS, A_DV), tok(A_HEADS, A_DV), tok(SUBLANES, LANES),
        pl.BlockSpec((None, None, A_HEADS, A_DK, A_DV), lambda b: (b, layer, 0, 0, 0)),
        pl.BlockSpec((None, None, 2, B_WINDOW, B_KV_HEADS, HEAD_DIM), lambda b: (b, layer, 0, 0, 0, 0)),
    ] + [pl.BlockSpec((None, None, 2, ATTN_BLOCK, None, C_HEADS, HEAD_DIM), lambda b: (b, layer, 0, 0, 0, 0, 0)) for _ in cc] + [
        pl.BlockSpec((LANES, A_HEADS * A_DK), lambda b: (0, 0)),
        pl.BlockSpec((1, A_HEADS * A_DK), lambda b: (0, 0)),
        pl.BlockSpec((1, A_DV), lambda b: (0, 0)),
    ]
    out_specs = [
        pl.BlockSpec((None, A_HEADS, A_DV), lambda b: (b, 0, 0)),
        pl.BlockSpec((None, 16, HEAD_DIM), lambda b: (b, 0, 0)),
        pl.BlockSpec((None, SUBLANES, HEAD_DIM), lambda b: (b, 0, 0)),
        pl.BlockSpec((None, A_HEADS, A_DK, A_DV), lambda b: (b, 0, 0, 0)),
    ]
    out_shape = [
        jax.ShapeDtypeStruct((bs, A_HEADS, A_DV), F32),
        jax.ShapeDtypeStruct((bs, 16, HEAD_DIM), F32),
        jax.ShapeDtypeStruct((bs, SUBLANES, HEAD_DIM), F32),
        jax.ShapeDtypeStruct((bs, A_HEADS, A_DK, A_DV), F32),
    ]
    ya, ob, yc, sn = pl.pallas_call(
        _sample_body, grid=(bs,), in_specs=in_specs, out_specs=out_specs, out_shape=out_shape,
        compiler_params=_cparams(("parallel",), 32), name="sample_mixers",
    )(sink, q_rows, kn_rows, vn_rows, qk_a, v_a, g_a, lr, state_gla, cache_b, *cc, wlr_pad, blr, ng)
    return (ya.reshape(bs, A_HEADS * A_DV), ob[:, :B_HEADS].reshape(bs, B_HEADS * HEAD_DIM),
            yc[:, :C_HEADS].reshape(bs, C_HEADS * HEAD_DIM), sn)


def _outproj_body(ya_ref, ob_ref, yc_ref, wa_ref, wb_ref, wc_ref, bias_ref, h_ref, y_ref):
    acc = jnp.dot(ya_ref[...].astype(BF16), wa_ref[...], preferred_element_type=F32)
    acc = acc + jnp.dot(ob_ref[...].astype(BF16), wb_ref[...], preferred_element_type=F32)
    acc = acc + jnp.dot(yc_ref[...].astype(BF16), wc_ref[...], preferred_element_type=F32)
    y_ref[...] = DN_ALPHA * h_ref[...] + acc + bias_ref[...]


def _outproj(ya, ob, yc, wa, wb, wc, bias, h, *, tm, tn=1024):
    m = h.shape[0]

    def rowspec(x):
        return pl.BlockSpec((tm, x.shape[1]), lambda j, i: (i, 0))

    def wspec(w):
        return pl.BlockSpec((w.shape[0], tn), lambda j, i: (0, j))

    return pl.pallas_call(
        _outproj_body, grid=(D_MODEL // tn, m // tm),
        in_specs=[rowspec(ya), rowspec(ob), rowspec(yc), wspec(wa), wspec(wb), wspec(wc),
                  pl.BlockSpec((1, tn), lambda j, i: (0, j)), pl.BlockSpec((tm, tn), lambda j, i: (i, j))],
        out_specs=pl.BlockSpec((tm, tn), lambda j, i: (i, j)),
        out_shape=jax.ShapeDtypeStruct((m, D_MODEL), F32),
        compiler_params=_cparams(("parallel", "parallel"), 48), name="outproj",
    )(ya, ob, yc, wa, wb, wc, bias, h)


def _ln_router_body(y_ref, g_ref, b_ref, rw_ref, rb_ref, h_ref, lg_ref):
    h = _ln_rows(y_ref[...], g_ref[...], b_ref[...])
    h_ref[...] = h
    lg_ref[...] = jnp.dot(h, rw_ref[...], precision=HIGHEST, preferred_element_type=F32) + rb_ref[...]


def _ln_router(y, g, b, rw_pad, rb_pad, tm):
    m, d = y.shape
    row = pl.BlockSpec((tm, d), lambda i: (i, 0))
    vec = pl.BlockSpec((1, d), lambda i: (0, 0))
    return pl.pallas_call(
        _ln_router_body, grid=(m // tm,),
        in_specs=[row, vec, vec, pl.BlockSpec((d, LANES), lambda i: (0, 0)), pl.BlockSpec((1, LANES), lambda i: (0, 0))],
        out_specs=[row, pl.BlockSpec((tm, LANES), lambda i: (i, 0))],
        out_shape=[jax.ShapeDtypeStruct((m, d), F32), jax.ShapeDtypeStruct((m, LANES), F32)],
        compiler_params=_cparams(("parallel",), 48), name="ln_router",
    )(y, g, b, rw_pad, rb_pad)


def _moe_gather_body(src_ref, na_ref, hp_hbm, hs_hbm, xs_ref, buf, sem, *, n_p):
    i = pl.program_id(0)
    na = na_ref[0]

    def issue(blk, slot):
        def body(r, carry):
            s = src_ref[blk * MOE_TM + r]
            dst = buf.at[slot, pl.ds(r, 1), :]

            @pl.when(s < n_p)
            def _():
                pltpu.make_async_copy(hp_hbm.at[pl.ds(s, 1), :], dst, sem.at[slot]).start()

            @pl.when(s >= n_p)
            def _():
                pltpu.make_async_copy(hs_hbm.at[pl.ds(s - n_p, 1), :], dst, sem.at[slot]).start()

            return carry
        lax.fori_loop(0, MOE_TM, body, 0)

    @pl.when(i == 0)
    def _():
        issue(0, 0)

    slot = i % 2

    @pl.when(i + 1 < na)
    def _():
        issue(i + 1, 1 - slot)

    @pl.when(i < na)
    def _():
        def wait_body(r, carry):
            pltpu.make_async_copy(hp_hbm.at[pl.ds(0, 1), :], buf.at[slot, pl.ds(r, 1), :], sem.at[slot]).wait()
            return carry
        lax.fori_loop(0, MOE_TM, wait_body, 0)
        xs_ref[...] = buf[slot].astype(BF16)

    @pl.when(i >= na)
    def _():
        xs_ref[...] = jnp.zeros_like(xs_ref)


def _moe_gather(h_p, h_s, src, n_active):
    n_p, d = h_p.shape
    n_blk = src.shape[0] // MOE_TM
    hbm = pl.BlockSpec(memory_space=pl.ANY)
    return pl.pallas_call(
        functools.partial(_moe_gather_body, n_p=n_p),
        grid_spec=pltpu.PrefetchScalarGridSpec(
            num_scalar_prefetch=2, grid=(n_blk,), in_specs=[hbm, hbm],
            out_specs=pl.BlockSpec((MOE_TM, d), lambda i, sr, na: (i, 0)),
            scratch_shapes=[pltpu.VMEM((2, MOE_TM, d), F32), pltpu.SemaphoreType.DMA((2,))]),
        out_shape=jax.ShapeDtypeStruct((n_blk * MOE_TM, d), BF16),
        compiler_params=_cparams(("arbitrary",), 32), name="moe_gather",
    )(src, n_active, h_p, h_s)


def _weight_stream(w_hbms, stages, sem, be_ref, nb_ref, na_ref, cnt_ref, *, layer, tn, on_ready):
    j = pl.program_id(0)
    i = pl.program_id(1)
    n_j = pl.num_programs(0)
    na = na_ref[0]
    active = i < na
    fresh = active & ((i == 0) | (be_ref[i] != be_ref[jnp.maximum(i - 1, 0)]))

    def copies(e, jj, slot):
        cols = pl.ds(pl.multiple_of(jj * tn, tn), tn)
        return [pltpu.make_async_copy(w.at[layer, e, :, cols], st.at[slot], sem.at[t, slot])
                for t, (w, st) in enumerate(zip(w_hbms, stages))]

    @pl.when((j == 0) & (i == 0))
    def _():
        cnt_ref[0] = 0
        for cp in copies(be_ref[0], 0, 0):
            cp.start()

    @pl.when(fresh)
    def _():
        slot = cnt_ref[0] % 2
        for cp in copies(be_ref[i], j, slot):
            cp.wait()
        nxt = nb_ref[i]
        same_j = nxt < na
        e_next = be_ref[jnp.where(same_j, nxt, 0)]
        j_next = jnp.where(same_j, j, j + 1)

        @pl.when(same_j | (j + 1 < n_j))
        def _():
            for cp in copies(e_next, j_next, 1 - slot):
                cp.start()

        on_ready(slot)
        cnt_ref[0] = cnt_ref[0] + 1

    return active


def _moe_up_body(be_ref, nb_ref, na_ref, x_ref, bg_ref, bu_ref, wg_hbm, wu_hbm, a_ref,
                 stage_g, stage_u, wg_bf, wu_bf, sem, cnt_ref, *, layer, tn):
    def on_ready(slot):
        wg_bf[...] = stage_g[slot].astype(BF16)
        wu_bf[...] = stage_u[slot].astype(BF16)

    active = _weight_stream((wg_hbm, wu_hbm), (stage_g, stage_u), sem, be_ref, nb_ref, na_ref, cnt_ref,
                            layer=layer, tn=tn, on_ready=on_ready)

    @pl.when(active)
    def _():
        x = x_ref[...]
        g = jnp.dot(x, wg_bf[...], preferred_element_type=F32) + bg_ref[...]
        u = jnp.dot(x, wu_bf[...], preferred_element_type=F32) + bu_ref[...]
        g = jnp.minimum(g, SWIGLU_LIMIT)
        u = jnp.clip(u, -SWIGLU_LIMIT, SWIGLU_LIMIT)
        a_ref[...] = (g * jax.nn.sigmoid(SWIGLU_ALPHA * g) * (u + 1.0)).astype(BF16)

    @pl.when(jnp.logical_not(active))
    def _():
        a_ref[...] = jnp.zeros_like(a_ref)


def _moe_down_body(be_ref, nb_ref, na_ref, a_ref, bd_ref, wd_hbm, y_ref, stage_d, wd_bf, sem, cnt_ref, *, layer, tn):
    def on_ready(slot):
        wd_bf[...] = stage_d[slot].astype(BF16)

    active = _weight_stream((wd_hbm,), (stage_d,), sem, be_ref, nb_ref, na_ref, cnt_ref,
                            layer=layer, tn=tn, on_ready=on_ready)

    @pl.when(active)
    def _():
        y_ref[...] = jnp.dot(a_ref[...], wd_bf[...], preferred_element_type=F32) + bd_ref[...]

    @pl.when(jnp.logical_not(active))
    def _():
        y_ref[...] = jnp.zeros_like(y_ref)


def _moe_experts(layer, xs, block_expert, next_block, n_active, w_gate, b_gate, w_up, b_up, w_down, b_down):
    m_pad, d = xs.shape
    d_ff = w_gate.shape[-1]
    n_blk = m_pad // MOE_TM

    def blk(i, na):
        return jnp.minimum(i, na[0] - 1)

    def rows(width):
        return pl.BlockSpec((MOE_TM, width), lambda j, i, be, nb, na: (blk(i, na), 0))

    def bias(t):
        return pl.BlockSpec((None, None, 1, t), lambda j, i, be, nb, na: (layer, be[i], 0, j))

    def out(t):
        return pl.BlockSpec((MOE_TM, t), lambda j, i, be, nb, na: (i, j))

    hbm = pl.BlockSpec(memory_space=pl.ANY)
    smem_cnt = pltpu.SMEM((1,), jnp.int32)

    tn = MOE_TN_UP
    act = pl.pallas_call(
        functools.partial(_moe_up_body, layer=layer, tn=tn),
        grid_spec=pltpu.PrefetchScalarGridSpec(
            num_scalar_prefetch=3, grid=(d_ff // tn, n_blk),
            in_specs=[rows(d), bias(tn), bias(tn), hbm, hbm],
            out_specs=out(tn),
            scratch_shapes=[pltpu.VMEM((2, d, tn), F32), pltpu.VMEM((2, d, tn), F32),
                            pltpu.VMEM((d, tn), BF16), pltpu.VMEM((d, tn), BF16),
                            pltpu.SemaphoreType.DMA((2, 2)), smem_cnt]),
        out_shape=jax.ShapeDtypeStruct((m_pad, d_ff), BF16),
        compiler_params=_cparams(("arbitrary", "arbitrary"), 56), name="moe_gate_up",
    )(block_expert, next_block, n_active, xs,
      b_gate.reshape(DEPTH, N_EXPERTS, 1, d_ff), b_up.reshape(DEPTH, N_EXPERTS, 1, d_ff), w_gate, w_up)

    tn2 = MOE_TN_DOWN
    return pl.pallas_call(
        functools.partial(_moe_down_body, layer=layer, tn=tn2),
        grid_spec=pltpu.PrefetchScalarGridSpec(
            num_scalar_prefetch=3, grid=(d // tn2, n_blk),
            in_specs=[rows(d_ff), bias(tn2), hbm],
            out_specs=out(tn2),
            scratch_shapes=[pltpu.VMEM((2, d_ff, tn2), F32), pltpu.VMEM((d_ff, tn2), BF16),
                            pltpu.SemaphoreType.DMA((1, 2)), smem_cnt]),
        out_shape=jax.ShapeDtypeStruct((m_pad, d), F32),
        compiler_params=_cparams(("arbitrary", "arbitrary"), 56), name="moe_down",
    )(block_expert, next_block, n_active, act, b_down.reshape(DEPTH, N_EXPERTS, 1, d), w_down)


def _combine_body(dest_ref, h_ref, gt_ref, g_ref, b_ref, yb_hbm, o_ref, ob_ref, yg, sem, *, tm):
    t = pl.program_id(0)
    n_tiles = pl.num_programs(0)

    def row_copy(tile, slot, r, kk):
        d = dest_ref[(tile * tm + r) * TOP_K + kk]
        return pltpu.make_async_copy(yb_hbm.at[pl.ds(d, 1), :], yg.at[slot, kk, pl.ds(r, 1), :], sem.at[slot])

    def issue(tile, slot):
        def body(r, carry):
            for kk in range(TOP_K):
                row_copy(tile, slot, r, kk).start()
            return carry
        lax.fori_loop(0, tm, body, 0)

    @pl.when(t == 0)
    def _():
        issue(0, 0)

    slot = t % 2

    @pl.when(t + 1 < n_tiles)
    def _():
        issue(t + 1, 1 - slot)

    def wait_body(r, carry):
        for kk in range(TOP_K):
            row_copy(t, slot, r, kk).wait()
        return carry
    lax.fori_loop(0, tm, wait_body, 0)

    gt = gt_ref[...]
    moe = ((yg[slot, 0] * gt[:, 0:1] + yg[slot, 1] * gt[:, 1:2])
           + (yg[slot, 2] * gt[:, 2:3] + yg[slot, 3] * gt[:, 3:4]))
    y = _ln_rows(DN_ALPHA * h_ref[...] + moe, g_ref[...], b_ref[...])
    o_ref[...] = y
    ob_ref[...] = y.astype(BF16)


def _combine_ln(h1, yb, dest, gates, g, b, tm):
    m, d = h1.shape
    row = pl.BlockSpec((tm, d), lambda i, dr: (i, 0))
    vec = pl.BlockSpec((1, d), lambda i, dr: (0, 0))
    return pl.pallas_call(
        functools.partial(_combine_body, tm=tm),
        grid_spec=pltpu.PrefetchScalarGridSpec(
            num_scalar_prefetch=1, grid=(m // tm,),
            in_specs=[row, pl.BlockSpec((tm, TOP_K), lambda i, dr: (i, 0)), vec, vec,
                      pl.BlockSpec(memory_space=pl.ANY)],
            out_specs=[row, row],
            scratch_shapes=[pltpu.VMEM((2, TOP_K, tm, d), F32), pltpu.SemaphoreType.DMA((2,))]),
        out_shape=[jax.ShapeDtypeStruct((m, d), F32), jax.ShapeDtypeStruct((m, d), BF16)],
        compiler_params=_cparams(("arbitrary",), 48), name="combine_ln2",
    )(dest, h1, gates, g, b, yb)


def _route(logits):
    n_tok = logits.shape[0]
    top_v, top_e = lax.top_k(logits, TOP_K)
    gates = jax.nn.softmax(top_v, axis=-1)
    m = n_tok * TOP_K
    e_flat = top_e.reshape(m)
    n_blk = -(-m // MOE_TM) + N_EXPERTS
    onehot = jax.nn.one_hot(e_flat, N_EXPERTS, dtype=jnp.int32)
    counts = jnp.sum(onehot, axis=0)
    first = jnp.cumsum(counts) - counts
    order = jnp.argsort(e_flat).astype(jnp.int32)
    pos = jnp.argsort(order).astype(jnp.int32)
    blocks_per = (counts + MOE_TM - 1) // MOE_TM
    blocks_end = jnp.cumsum(blocks_per)
    row0 = (blocks_end - blocks_per) * MOE_TM
    dest = pos + jnp.sum(onehot * (row0 - first)[None, :], axis=1)
    n_active = blocks_end[-1]
    bidx = jnp.minimum(jnp.arange(n_blk), n_active - 1)
    block_expert = jnp.minimum(jnp.sum(bidx[:, None] >= blocks_end[None, :], axis=1), N_EXPERTS - 1).astype(jnp.int32)
    next_block = blocks_end[block_expert].astype(jnp.int32)
    off = (jnp.arange(n_blk)[:, None] * MOE_TM - row0[block_expert][:, None]) + jnp.arange(MOE_TM)[None, :]
    valid = (off < counts[block_expert][:, None]) & (jnp.arange(n_blk)[:, None] < n_active)
    sorted_pos = jnp.clip(first[block_expert][:, None] + off, 0, m - 1)
    src = jnp.where(valid, order[sorted_pos] // TOP_K, 0).reshape(n_blk * MOE_TM)
    return gates, dest.astype(jnp.int32), src, block_expert, next_block, n_active.reshape(1).astype(jnp.int32)


def _rope_tables(pos):
    half = HEAD_DIM // 2
    inv_freq = ROPE_THETA ** (-np.arange(half, dtype=np.float64) / half)
    ang = np.asarray(pos, np.float64)[:, None] * inv_freq[None, :]
    cos, sin = np.cos(ang), np.sin(ang)
    return (jnp.asarray(np.concatenate([cos, cos], axis=1), F32),
            jnp.asarray(np.concatenate([-sin, sin], axis=1), F32))


def _split_w_in(w, b):
    k = w.shape[0]
    w = w.astype(BF16)
    w_r = jnp.concatenate([w[:, _O_QB:_O_VB], w[:, _O_QC:_O_VC], jnp.zeros((k, LANES), w.dtype)], axis=1)
    w_p = jnp.concatenate([w[:, _O_QA:_O_LR], w[:, _O_VB:_O_QC], w[:, _O_VC:_O_END], w[:, _O_LR:_O_QB],
                           jnp.zeros((k, LANES - A_LOWRANK), w.dtype)], axis=1)
    b_r = jnp.concatenate([b[_O_QB:_O_VB], b[_O_QC:_O_VC], jnp.zeros((LANES,), b.dtype)])
    b_p = jnp.concatenate([b[_O_QA:_O_LR], b[_O_VB:_O_QC], b[_O_VC:_O_END], b[_O_LR:_O_QB],
                           jnp.zeros((LANES - A_LOWRANK,), b.dtype)])
    return w_r, w_p, b_r.reshape(1, W_PROJ), b_p.reshape(1, W_PROJ)


def _mix_prompt(rp, pp, wlr_pad, blr, ng, sink, bp, tlen):
    ya, st = _gla_prompt(pp, wlr_pad, blr, ng, bp, tlen)
    ob = _swa_attention(rp, pp, sink, bp, tlen)
    yc = _dilated_attention(rp, pp, bp, tlen)
    return ya, ob, yc, st


def kernel(x_prompt, x_sample, cache_b_kv, cache_c1_kv, cache_c2_kv, cache_c3_kv, state_gla, ln_in_g, ln_in_b, w_in, b_in, gla_w_gate, gla_b_gate, gla_norm_g, attn_sinks, w_out, b_out, ln1_g, ln1_b, router_w, router_b, w_gate, b_gate, w_up, b_up, w_down, b_down, ln2_g, ln2_b):
    bp, tlen, d = x_prompt.shape
    bs, slen, _ = x_sample.shape
    assert slen == 1 and d == D_MODEL and tlen % SWA_ROWS == 0
    n_p, n_s = bp * tlen, bs * slen
    tm_p = 256

    cos_p, sin_p = _rope_tables(np.tile(np.arange(tlen), bp))
    cos_s, sin_s = _rope_tables(np.full((n_s,), PAST_LEN))
    caches_c = (cache_c1_kv, cache_c2_kv, cache_c3_kv)

    hp, hp_b = _ln_in(x_prompt.reshape(n_p, d), ln_in_g, ln_in_b, tm_p)
    hs, hs_b = _ln_in(x_sample.reshape(n_s, d), ln_in_g, ln_in_b, n_s)

    new_p = [[] for _ in range(5)]
    new_s = [[] for _ in range(5)]
    for l in range(DEPTH):
        w_r, w_p, b_r, b_p = _split_w_in(w_in[l], b_in[l])
        wlr_pad = jnp.concatenate([gla_w_gate[l], jnp.zeros((LANES - A_LOWRANK, A_HEADS * A_DK), F32)], axis=0)
        blr = gla_b_gate[l].reshape(1, A_HEADS * A_DK)
        ng = gla_norm_g[l].reshape(1, A_DV)
        sink = attn_sinks[l].reshape(B_HEADS)
        wo = w_out[l].astype(BF16)
        na, nb = A_HEADS * A_DV, B_HEADS * HEAD_DIM
        wo_a, wo_b, wo_c = wo[:na], wo[na:na + nb], wo[na + nb:]
        bo = b_out[l].reshape(1, d)

        rp = _proj(hp_b, w_r, b_r, cos_p, sin_p, tm=512)
        pp = _proj(hp_b, w_p, b_p, tm=512)
        rs = _proj(hs_b, w_r, b_r, cos_s, sin_s, tm=n_s, tn=W_PROJ // 2)
        ps = _proj(hs_b, w_p, b_p, tm=n_s, tn=W_PROJ // 2)

        ya_p, ob_p, yc_p, st_p = _mix_prompt(rp, pp, wlr_pad, blr, ng, sink, bp, tlen)
        ya_s, ob_s, yc_s, st_s = _sample_mixers(l, rs, ps, state_gla, cache_b_kv, caches_c, sink, wlr_pad, blr, ng)

        rp4 = rp.reshape(bp, tlen, W_PROJ)
        pp4 = pp.reshape(bp, tlen, W_PROJ)

        def kv_prompt(uk, uv, heads, rows):
            kk = rp4[:, tlen - rows:, uk * LANES:(uk + heads) * LANES].reshape(bp, rows, heads, HEAD_DIM)
            vv = pp4[:, tlen - rows:, uv * LANES:(uv + heads) * LANES].reshape(bp, rows, heads, HEAD_DIM)
            return jnp.stack([kk, vv], axis=1)

        def kv_sample(uk, uv, heads):
            kk = rs[:, uk * LANES:(uk + heads) * LANES].reshape(bs, slen, heads, HEAD_DIM)
            vv = ps[:, uv * LANES:(uv + heads) * LANES].reshape(bs, slen, heads, HEAD_DIM)
            return jnp.stack([kk, vv], axis=1)

        new_p[0].append(kv_prompt(R_KB, P_VB, B_KV_HEADS, min(B_WINDOW, tlen)))
        new_s[0].append(kv_sample(R_KB, P_VB, B_KV_HEADS))
        for g, (win, dil) in enumerate(C_PATTERNS):
            new_p[1 + g].append(kv_prompt(R_KC + C_HEADS * g, P_VC + C_HEADS * g, C_HEADS, min(win, tlen)))
            new_s[1 + g].append(kv_sample(R_KC + C_HEADS * g, P_VC + C_HEADS * g, C_HEADS))
        new_p[4].append(st_p)
        new_s[4].append(st_s)

        y_p = _outproj(ya_p, ob_p, yc_p, wo_a, wo_b, wo_c, bo, hp, tm=512)
        y_s = _outproj(ya_s, ob_s, yc_s, wo_a, wo_b, wo_c, bo, hs, tm=n_s, tn=D_MODEL // 2)
        rw_pad = jnp.concatenate([router_w[l], jnp.zeros((d, LANES - N_EXPERTS), F32)], axis=1)
        rb_pad = jnp.concatenate([router_b[l], jnp.zeros((LANES - N_EXPERTS,), F32)]).reshape(1, LANES)
        g1, b1 = ln1_g[l].reshape(1, d), ln1_b[l].reshape(1, d)
        h1_p, lg_p = _ln_router(y_p, g1, b1, rw_pad, rb_pad, tm_p)
        h1_s, lg_s = _ln_router(y_s, g1, b1, rw_pad, rb_pad, n_s)

        logits = jnp.concatenate([lg_p, lg_s], axis=0)[:, :N_EXPERTS]
        gates, dest, src, block_expert, next_block, n_active = _route(logits)
        xs = _moe_gather(h1_p, h1_s, src, n_active)
        yb = _moe_experts(l, xs, block_expert, next_block, n_active, w_gate, b_gate, w_up, b_up, w_down, b_down)
        g2, b2 = ln2_g[l].reshape(1, d), ln2_b[l].reshape(1, d)
        hp, hp_b = _combine_ln(h1_p, yb, dest[:n_p * TOP_K], gates[:n_p], g2, b2, 128)
        hs, hs_b = _combine_ln(h1_s, yb, dest[n_p * TOP_K:], gates[n_p:], g2, b2, n_s)

    outs = [hp.reshape(bp, tlen, d), hs.reshape(bs, slen, d)]
    for i in range(5):
        outs += [jnp.stack(new_p[i], axis=1), jnp.stack(new_s[i], axis=1)]
    return tuple(outs)
```

```python
import functools

import numpy as np
import jax
import jax.numpy as jnp
from jax import lax
from jax.experimental import pallas as pl
from jax.experimental.pallas import tpu as pltpu

F32 = jnp.float32
BF16 = jnp.bfloat16
HIGHEST = lax.Precision.HIGHEST

LANES = 128
SUBLANES = 8

D_MODEL = 4096
DEPTH = 2
PAST_LEN = 16384
HEAD_DIM = 128
A_HEADS, A_DK, A_DV, A_LOWRANK = 4, 128, 256, 16
A_GATE_NORM = 16.0
A_CHUNK = 64
B_HEADS, B_KV_HEADS = 12, 3
B_GROUP = B_HEADS // B_KV_HEADS
B_WINDOW = 128
C_PATTERNS = ((128, 1), (512, 4), (2048, 16))
N_C_GROUPS = len(C_PATTERNS)
C_HEADS = 4
ATTN_BLOCK = 128
ROPE_THETA = 10000.0
N_EXPERTS = 32
TOP_K = 4
SWIGLU_LIMIT = 7.0
SWIGLU_ALPHA = 1.702
DN_ALPHA = (2 * DEPTH) ** 0.25
LN_EPS = 1e-5
RMS_EPS = 1e-6
NEG_INF = -1e30
ATTN_SCALE = HEAD_DIM ** -0.5

_O_QA, _O_KA, _O_VA, _O_GA, _O_LR, _O_QB, _O_KB, _O_VB, _O_QC, _O_KC, _O_VC, _O_END = (
    0, 512, 1024, 2048, 3072, 3088, 4624, 5008, 5392, 6928, 8464, 10000)
R_QB, R_KB, R_QC, R_KC = 0, 12, 15, 27
P_QA, P_KA, P_VA, P_GA, P_VB, P_VC, P_LR = 0, 4, 8, 16, 24, 27, 39
U_PROJ = 40
W_PROJ = U_PROJ * LANES

S_ROWS = 16 + SUBLANES * N_C_GROUPS

MOE_TN_UP = 512
MOE_TN_DOWN = 1024
SWA_ROWS = 512


def _cparams(sem, vmem_mb):
    return pltpu.CompilerParams(dimension_semantics=sem, vmem_limit_bytes=vmem_mb * 1024 * 1024)


def _ln_rows(x, g, b):
    mu = jnp.mean(x, axis=-1, keepdims=True)
    xc = x - mu
    var = jnp.mean(xc * xc, axis=-1, keepdims=True)
    return xc * lax.rsqrt(var + LN_EPS) * g + b


def _ln_in_body(x_ref, g_ref, b_ref, o_ref, ob_ref):
    y = _ln_rows(x_ref[...], g_ref[...], b_ref[...])
    o_ref[...] = y
    ob_ref[...] = y.astype(BF16)


def _ln_in(x, g, b, tm):
    m, d = x.shape
    row = pl.BlockSpec((tm, d), lambda i: (i, 0))
    vec = pl.BlockSpec((1, d), lambda i: (0, 0))
    return pl.pallas_call(
        _ln_in_body, grid=(m // tm,), in_specs=[row, vec, vec], out_specs=[row, row],
        out_shape=[jax.ShapeDtypeStruct((m, d), F32), jax.ShapeDtypeStruct((m, d), BF16)],
        compiler_params=_cparams(("parallel",), 48), name="ln_in",
    )(x, g.reshape(1, d), b.reshape(1, d))


def _proj_body(*refs, rope, tn):
    if rope:
        x_ref, w_ref, b_ref, cos_ref, sin_ref, o_ref = refs
    else:
        x_ref, w_ref, b_ref, o_ref = refs
    acc = jnp.dot(x_ref[...], w_ref[...], preferred_element_type=F32) + b_ref[...]
    if not rope:
        o_ref[...] = acc
        return
    cos = cos_ref[...]
    sin = sin_ref[...]
    for u in range(tn // LANES):
        seg = acc[:, u * LANES:(u + 1) * LANES]
        o_ref[:, u * LANES:(u + 1) * LANES] = seg * cos + pltpu.roll(seg, HEAD_DIM // 2, 1) * sin


def _proj(xb, w, b, cos=None, sin=None, *, tm, tn=1024):
    m, k = xb.shape
    n = w.shape[1]
    rope = cos is not None
    in_specs = [pl.BlockSpec((tm, k), lambda j, i: (i, 0)),
                pl.BlockSpec((k, tn), lambda j, i: (0, j)),
                pl.BlockSpec((1, tn), lambda j, i: (0, j))]
    args = [xb, w, b]
    if rope:
        tab = pl.BlockSpec((tm, LANES), lambda j, i: (i, 0))
        in_specs += [tab, tab]
        args += [cos, sin]
    return pl.pallas_call(
        functools.partial(_proj_body, rope=rope, tn=tn), grid=(n // tn, m // tm),
        in_specs=in_specs, out_specs=pl.BlockSpec((tm, tn), lambda j, i: (i, j)),
        out_shape=jax.ShapeDtypeStruct((m, n), F32),
        compiler_params=_cparams(("parallel", "parallel"), 48),
        name="proj_rope" if rope else "proj_plain",
    )(*args)


def _log_sigmoid(z):
    return jnp.minimum(z, 0.0) - jnp.log(1.0 + jnp.exp(-jnp.abs(z)))


def _rms_gate(o, ng, ga):
    o = o * lax.rsqrt(jnp.mean(o * o, axis=-1, keepdims=True) + RMS_EPS) * ng
    return o * (ga * jax.nn.sigmoid(ga))


def _gla_body(q_ref, k_ref, v_ref, ga_ref, lr_ref, wlr_ref, blr_ref, ng_ref, ya_ref, st_ref, s_scr, *, n_sub):
    c = pl.program_id(2)

    @pl.when(c == 0)
    def _():
        s_scr[...] = jnp.zeros_like(s_scr)

    ch = A_CHUNK
    rr = lax.broadcasted_iota(jnp.int32, (ch, ch), 0)
    cc = lax.broadcasted_iota(jnp.int32, (ch, ch), 1)
    causal = rr >= cc
    tri = causal.astype(F32)
    ones = jnp.ones((ch, A_DK), F32)
    for u in range(n_sub):
        sl = pl.ds(u * ch, ch)
        z = jnp.dot(lr_ref[sl, :], wlr_ref[...], precision=HIGHEST, preferred_element_type=F32) + blr_ref[...]
        g = _log_sigmoid(z) / A_GATE_NORM
        b = jnp.dot(tri, g, precision=HIGHEST, preferred_element_type=F32)
        b_last = b[ch - 1:ch, :]
        b_last_col = lax.dot_general(g, ones, (((0,), (0,)), ((), ())), precision=HIGHEST,
                                     preferred_element_type=F32)
        q = q_ref[sl, :] * (A_DK ** -0.5)
        k = k_ref[sl, :]
        v = v_ref[sl, :].astype(BF16)
        qg = (q * jnp.exp(b)).astype(BF16)
        kg = (k * jnp.exp(-b)).astype(BF16)
        kd = (k * jnp.exp(b_last - b)).astype(BF16)
        att = lax.dot_general(qg, kg, (((1,), (1,)), ((), ())), preferred_element_type=F32)
        att = jnp.where(causal, att, 0.0).astype(BF16)
        s = s_scr[...]
        o = (jnp.dot(qg, s.astype(BF16), preferred_element_type=F32)
             + jnp.dot(att, v, preferred_element_type=F32))
        decay = jnp.exp(b_last_col)
        decay = jnp.concatenate([decay, decay], axis=1)
        s_scr[...] = decay * s + lax.dot_general(kd, v, (((0,), (0,)), ((), ())), preferred_element_type=F32)
        ya_ref[sl, :] = _rms_gate(o, ng_ref[...], ga_ref[sl, :])

    @pl.when(c == pl.num_programs(2) - 1)
    def _():
        st_ref[...] = s_scr[...]


def _gla_prompt(p_arr, wlr_pad, blr, ng, bsz, tlen, rb=256):
    nblk = tlen // rb
    n_sub = rb // A_CHUNK

    def rows(b, c):
        return b * nblk + c

    in_specs = [
        pl.BlockSpec((rb, A_DK), lambda b, h, c: (rows(b, c), P_QA + h)),
        pl.BlockSpec((rb, A_DK), lambda b, h, c: (rows(b, c), P_KA + h)),
        pl.BlockSpec((rb, A_DV), lambda b, h, c: (rows(b, c), P_VA // 2 + h)),
        pl.BlockSpec((rb, A_DV), lambda b, h, c: (rows(b, c), P_GA // 2 + h)),
        pl.BlockSpec((rb, LANES), lambda b, h, c: (rows(b, c), P_LR)),
        pl.BlockSpec((LANES, A_DK), lambda b, h, c: (0, h)),
        pl.BlockSpec((1, A_DK), lambda b, h, c: (0, h)),
        pl.BlockSpec((1, A_DV), lambda b, h, c: (0, 0)),
    ]
    out_specs = [
        pl.BlockSpec((rb, A_DV), lambda b, h, c: (rows(b, c), h)),
        pl.BlockSpec((None, None, A_DK, A_DV), lambda b, h, c: (b, h, 0, 0)),
    ]
    return pl.pallas_call(
        functools.partial(_gla_body, n_sub=n_sub), grid=(bsz, A_HEADS, nblk),
        in_specs=in_specs, out_specs=out_specs,
        out_shape=[jax.ShapeDtypeStruct((bsz * tlen, A_HEADS * A_DV), F32),
                   jax.ShapeDtypeStruct((bsz, A_HEADS, A_DK, A_DV), F32)],
        scratch_shapes=[pltpu.VMEM((A_DK, A_DV), F32)],
        compiler_params=_cparams(("parallel", "parallel", "arbitrary"), 32), name="gla_prompt",
    )(p_arr, p_arr, p_arr, p_arr, p_arr, wlr_pad, blr, ng)


def _attn_block(q16, k_prev, k_cur, v_prev, v_cur, prev_on, sink):
    blk = ATTN_BLOCK
    rows = q16.shape[0]
    nk = blk if k_prev is None else 2 * blk
    r = lax.broadcasted_iota(jnp.int32, (rows, nk), 0)
    if rows > blk:
        r = r % blk
    c = lax.broadcasted_iota(jnp.int32, (rows, nk), 1)
    if k_prev is None:
        keys, vals = k_cur.astype(BF16), v_cur.astype(BF16)
        mask = c <= r
    else:
        keys = jnp.concatenate([k_prev, k_cur], axis=0).astype(BF16)
        vals = jnp.concatenate([v_prev, v_cur], axis=0).astype(BF16)
        in_prev = (c < blk) & (c >= r)
        if prev_on is not True:
            in_prev = in_prev & prev_on
        mask = in_prev | ((c >= blk) & (c - blk <= r))
    s = lax.dot_general(q16, keys, (((1,), (1,)), ((), ())), preferred_element_type=F32) * ATTN_SCALE
    s = jnp.where(mask, s, NEG_INF)
    m = jnp.max(s, axis=1, keepdims=True)
    if sink is not None:
        m = jnp.maximum(m, sink)
    p = jnp.exp(s - m)
    den = jnp.sum(p, axis=1, keepdims=True)
    if sink is not None:
        den = den + jnp.exp(sink - m)
    o = jnp.dot(p.astype(BF16), vals, preferred_element_type=F32) / den
    return o, m + jnp.log(den)


def _swa_body(sink_ref, q_ref, kc_ref, vc_ref, kp_ref, vp_ref, o_ref):
    blk = ATTN_BLOCK
    hk = pl.program_id(1)
    n = pl.program_id(2)
    sink = jnp.concatenate([jnp.full((blk, 1), sink_ref[hk * B_GROUP + g], F32) for g in range(B_GROUP)], axis=0)
    for i in range(SWA_ROWS // blk):
        rows = slice(i * blk, (i + 1) * blk)
        q = q_ref[rows, :]
        q16 = jnp.concatenate([q[:, g * HEAD_DIM:(g + 1) * HEAD_DIM] for g in range(B_GROUP)], axis=0).astype(BF16)
        if i == 0:
            k_prev, v_prev, prev_on = kp_ref[...], vp_ref[...], n > 0
        else:
            prev = slice((i - 1) * blk, i * blk)
            k_prev, v_prev, prev_on = kc_ref[prev, :], vc_ref[prev, :], True
        o, _ = _attn_block(q16, k_prev, kc_ref[rows, :], v_prev, vc_ref[rows, :], prev_on, sink)
        o_ref[rows, :] = jnp.concatenate([o[g * blk:(g + 1) * blk, :] for g in range(B_GROUP)], axis=1)


def _swa_attention(r_arr, p_arr, sink, bsz, tlen):
    nsb = tlen // SWA_ROWS
    sub = SWA_ROWS // ATTN_BLOCK
    gw = B_GROUP * HEAD_DIM

    def prev_blk(b, n):
        return (b * nsb + n) * sub - jnp.minimum(n, 1)

    in_specs = [
        pl.BlockSpec(memory_space=pltpu.SMEM),
        pl.BlockSpec((SWA_ROWS, gw), lambda b, h, n: (b * nsb + n, R_QB // B_GROUP + h)),
        pl.BlockSpec((SWA_ROWS, HEAD_DIM), lambda b, h, n: (b * nsb + n, R_KB + h)),
        pl.BlockSpec((SWA_ROWS, HEAD_DIM), lambda b, h, n: (b * nsb + n, P_VB + h)),
        pl.BlockSpec((ATTN_BLOCK, HEAD_DIM), lambda b, h, n: (prev_blk(b, n), R_KB + h)),
        pl.BlockSpec((ATTN_BLOCK, HEAD_DIM), lambda b, h, n: (prev_blk(b, n), P_VB + h)),
    ]
    return pl.pallas_call(
        _swa_body, grid=(bsz, B_KV_HEADS, nsb), in_specs=in_specs,
        out_specs=pl.BlockSpec((SWA_ROWS, gw), lambda b, h, n: (b * nsb + n, h)),
        out_shape=jax.ShapeDtypeStruct((bsz * tlen, B_HEADS * HEAD_DIM), F32),
        compiler_params=_cparams(("parallel", "parallel", "parallel"), 32), name="swa_attn",
    )(sink, r_arr, r_arr, p_arr, r_arr, p_arr)


def _dilated_body(*refs, tlen):
    qkv = refs[:3 * N_C_GROUPS]
    yc_ref = refs[3 * N_C_GROUPS]
    o_scr = refs[3 * N_C_GROUPS + 1:3 * N_C_GROUPS + 1 + N_C_GROUPS]
    l_scr = refs[3 * N_C_GROUPS + 1 + N_C_GROUPS:]
    blk = ATTN_BLOCK
    for g, (win, dil) in enumerate(C_PATTERNS):
        q_ref, k_ref, v_ref = qkv[3 * g:3 * g + 3]
        for res in range(dil):
            for i in range(tlen // dil // blk):
                def rows(ii):
                    start = res + dil * blk * ii
                    return pl.ds(start, blk) if dil == 1 else pl.ds(start, blk, stride=dil)
                cur = rows(i)
                if i == 0:
                    k_prev = v_prev = None
                else:
                    k_prev, v_prev = k_ref[rows(i - 1), :], v_ref[rows(i - 1), :]
                o, lse = _attn_block(q_ref[cur, :].astype(BF16), k_prev, k_ref[cur, :], v_prev, v_ref[cur, :], True, None)
                o_scr[g][cur, :] = o
                l_scr[g][cur, :] = jnp.broadcast_to(lse, (blk, HEAD_DIM))
    step = 2 * blk
    for ch in range(tlen // step):
        rows = slice(ch * step, (ch + 1) * step)
        ls = [l[rows, :] for l in l_scr]
        lmax = jnp.maximum(jnp.maximum(ls[0], ls[1]), ls[2])
        es = [jnp.exp(l - lmax) for l in ls]
        num = es[0] * o_scr[0][rows, :] + es[1] * o_scr[1][rows, :] + es[2] * o_scr[2][rows, :]
        yc_ref[rows, :] = num / (es[0] + es[1] + es[2])


def _dilated_attention(r_arr, p_arr, bsz, tlen):
    in_specs, args = [], []
    for g, (win, dil) in enumerate(C_PATTERNS):
        assert win // dil == ATTN_BLOCK and tlen % (dil * ATTN_BLOCK) == 0
        in_specs += [pl.BlockSpec((tlen, HEAD_DIM), lambda b, h, g=g: (b, R_QC + C_HEADS * g + h)),
                     pl.BlockSpec((tlen, HEAD_DIM), lambda b, h, g=g: (b, R_KC + C_HEADS * g + h)),
                     pl.BlockSpec((tlen, HEAD_DIM), lambda b, h, g=g: (b, P_VC + C_HEADS * g + h))]
        args += [r_arr, r_arr, p_arr]
    return pl.pallas_call(
        functools.partial(_dilated_body, tlen=tlen), grid=(bsz, C_HEADS), in_specs=in_specs,
        out_specs=pl.BlockSpec((tlen, HEAD_DIM), lambda b, h: (b, h)),
        out_shape=jax.ShapeDtypeStruct((bsz * tlen, C_HEADS * HEAD_DIM), F32),
        scratch_shapes=[pltpu.VMEM((tlen, HEAD_DIM), F32) for _ in range(2 * N_C_GROUPS)],
        compiler_params=_cparams(("parallel", "parallel"), 40), name="dilated_attn",
    )(*args)


def _sample_body(sink_ref, q_ref, kn_ref, vn_ref, qk_ref, va_ref, ga_ref, lr_ref, st_ref,
                 cb_ref, c1_ref, c2_ref, c3_ref, wlr_ref, blr_ref, ng_ref,
                 ya_ref, ob_ref, yc_ref, sn_ref):
    nt = (((1,), (1,)), ((), ()))

    z = jnp.dot(lr_ref[...], wlr_ref[...], precision=HIGHEST, preferred_element_type=F32)[0:1, :] + blr_ref[...]
    a_all = jnp.exp(_log_sigmoid(z) / A_GATE_NORM)
    qk = qk_ref[...]
    rid16 = lax.broadcasted_iota(jnp.int32, (16, 1), 0)
    rows = jnp.zeros((16, A_DK), F32)
    for h in range(A_HEADS):
        a_h = jnp.broadcast_to(a_all[:, h * A_DK:(h + 1) * A_DK], (16, A_DK))
        k_h = jnp.broadcast_to(qk[A_HEADS + h:A_HEADS + h + 1, :], (16, A_DK))
        q_h = jnp.broadcast_to(qk[h:h + 1, :], (16, A_DK)) * (A_DK ** -0.5)
        rows = jnp.where(rid16 == 3 * h, a_h, rows)
        rows = jnp.where(rid16 == 3 * h + 1, k_h, rows)
        rows = jnp.where(rid16 == 3 * h + 2, q_h, rows)
    cols = jnp.concatenate([rows, jnp.zeros((LANES - 16, A_DK), F32)], axis=0).T
    for h in range(A_HEADS):
        a_col, k_col, q_col = cols[:, 3 * h:3 * h + 1], cols[:, 3 * h + 1:3 * h + 2], cols[:, 3 * h + 2:3 * h + 3]
        s_new = a_col * st_ref[h] + k_col * va_ref[h:h + 1, :]
        sn_ref[h] = s_new
        o = jnp.sum(q_col * s_new, axis=0, keepdims=True)
        ya_ref[h:h + 1, :] = _rms_gate(o, ng_ref[...], ga_ref[h:h + 1, :])

    q = q_ref[...]
    q16 = q.astype(BF16)
    rid = lax.broadcasted_iota(jnp.int32, (S_ROWS, 1), 0)

    def pick_b(j):
        return (rid < B_HEADS) & (rid // B_GROUP == j)

    def pick_c(g, hh):
        return rid == 16 + SUBLANES * g + hh

    c_refs = (c1_ref, c2_ref, c3_ref)
    s = jnp.zeros((S_ROWS, ATTN_BLOCK), F32)
    for j in range(B_KV_HEADS):
        kj = cb_ref[0, :, j, :].astype(BF16)
        s = jnp.where(pick_b(j), lax.dot_general(q16, kj, nt, preferred_element_type=F32), s)
    for g in range(N_C_GROUPS):
        for hh in range(C_HEADS):
            kh = c_refs[g][0, :, hh, :].astype(BF16)
            s = jnp.where(pick_c(g, hh), lax.dot_general(q16, kh, nt, preferred_element_type=F32), s)
    s = s * ATTN_SCALE
    s_n = jnp.sum(q * kn_ref[...], axis=1, keepdims=True) * ATTN_SCALE
    sink = jnp.full((S_ROWS, 1), NEG_INF, F32)
    for i in range(B_HEADS):
        sink = jnp.where(rid == i, sink_ref[i], sink)
    m = jnp.maximum(jnp.maximum(jnp.max(s, axis=1, keepdims=True), s_n), sink)
    p = jnp.exp(s - m)
    p_n = jnp.exp(s_n - m)
    den = jnp.sum(p, axis=1, keepdims=True) + p_n + jnp.exp(sink - m)
    p16 = p.astype(BF16)
    acc = p_n * vn_ref[...]
    for j in range(B_KV_HEADS):
        vj = cb_ref[1, :, j, :].astype(BF16)
        acc = acc + jnp.where(pick_b(j), jnp.dot(p16, vj, preferred_element_type=F32), 0.0)
    for g in range(N_C_GROUPS):
        for hh in range(C_HEADS):
            vh = c_refs[g][1, :, hh, :].astype(BF16)
            acc = acc + jnp.where(pick_c(g, hh), jnp.dot(p16, vh, preferred_element_type=F32), 0.0)
    o = acc / den
    lse = m + jnp.log(den)
    ob_ref[...] = o[0:16, :]
    ls = [lse[16 + SUBLANES * g:24 + SUBLANES * g, :] for g in range(N_C_GROUPS)]
    os_ = [o[16 + SUBLANES * g:24 + SUBLANES * g, :] for g in range(N_C_GROUPS)]
    lmax = jnp.maximum(jnp.maximum(ls[0], ls[1]), ls[2])
    es = [jnp.exp(l - lmax) for l in ls]
    yc_ref[...] = (es[0] * os_[0] + es[1] * os_[1] + es[2] * os_[2]) / (es[0] + es[1] + es[2])


def _sample_mixers(layer, r_s, p_s, state_gla, cache_b, caches_c, sink, wlr_pad, blr, ng):
    bs = r_s.shape[0]
    r3 = r_s.reshape(bs, U_PROJ, LANES)
    p3 = p_s.reshape(bs, U_PROJ, LANES)
    z4 = jnp.zeros((bs, 4, LANES), F32)

    def qrows(x3, ub, uc, rep):
        parts = [jnp.repeat(x3[:, ub:ub + B_KV_HEADS], B_GROUP, axis=1) if rep else x3[:, ub:ub + B_HEADS], z4]
        for g in range(N_C_GROUPS):
            parts += [x3[:, uc + C_HEADS * g:uc + C_HEADS * (g + 1)], z4]
        return jnp.concatenate(parts, axis=1)

    q_rows = qrows(r3, R_QB, R_QC, False)
    kn_rows = qrows(r3, R_KB, R_KC, True)
    vn_rows = qrows(p3, P_VB, P_VC, True)
    qk_a = p3[:, P_QA:P_VA]
    v_a = p_s[:, P_VA * LANES:P_GA * LANES].reshape(bs, A_HEADS, A_DV)
    g_a = p_s[:, P_GA * LANES:P_VB * LANES].reshape(bs, A_HEADS, A_DV)
    lr = jnp.broadcast_to(p3[:, P_LR:P_LR + 1], (bs, SUBLANES, LANES))

    cc = [c.reshape(bs, DEPTH, 2, ATTN_BLOCK, dil, C_HEADS, HEAD_DIM) for c, (_, dil) in zip(caches_c, C_PATTERNS)]

    def tok(rows, width):
        return pl.BlockSpec((None, rows, width), lambda b: (b, 0, 0))

    in_specs = [
        pl.BlockSpec(memory_space=pltpu.SMEM),
        tok(S_ROWS, LANES), tok(S_ROWS, LANES), tok(S_ROWS, LANES),
        tok(2 * A_HEADS, LANES), tok(A_HEADS, A_DV), tok(A_HEADS, A_DV), tok(SUBLANES, LANES),
        pl.BlockSpec((None, None, A_HEADS, A_DK, A_DV), lambda b: (b, layer, 0, 0, 0)),
        pl.BlockSpec((None, None, 2, B_WINDOW, B_KV_HEADS, HEAD_DIM), lambda b: (b, layer, 0, 0, 0, 0)),
    ] + [pl.BlockSpec((None, None, 2, ATTN_BLOCK, None, C_HEADS, HEAD_DIM), lambda b: (b, layer, 0, 0, 0, 0, 0)) for _ in cc] + [
        pl.BlockSpec((LANES, A_HEADS * A_DK), lambda b: (0, 0)),
        pl.BlockSpec((1, A_HEADS * A_DK), lambda b: (0, 0)),
        pl.BlockSpec((1, A_DV), lambda b: (0, 0)),
    ]
    out_specs = [
        pl.BlockSpec((None, A_HEADS, A_DV), lambda b: (b, 0, 0)),
        pl.BlockSpec((None, 16, HEAD_DIM), lambda b: (b, 0, 0)),
        pl.BlockSpec((None, SUBLANES, HEAD_DIM), lambda b: (b, 0, 0)),
        pl.BlockSpec((None, A_HEADS, A_DK, A_DV), lambda b: (b, 0, 0, 0)),
    ]
    out_shape = [
        jax.ShapeDtypeStruct((bs, A_HEADS, A_DV), F32),
        jax.ShapeDtypeStruct((bs, 16, HEAD_DIM), F32),
        jax.ShapeDtypeStruct((bs, SUBLANES, HEAD_DIM), F32),
        jax.ShapeDtypeStruct((bs, A_HEADS, A_DK, A_DV), F32),
    ]
    ya, ob, yc, sn = pl.pallas_call(
        _sample_body, grid=(bs,), in_specs=in_specs, out_specs=out_specs, out_shape=out_shape,
        compiler_params=_cparams(("parallel",), 32), name="sample_mixers",
    )(sink, q_rows, kn_rows, vn_rows, qk_a, v_a, g_a, lr, state_gla, cache_b, *cc, wlr_pad, blr, ng)
    return (ya.reshape(bs, A_HEADS * A_DV), ob[:, :B_HEADS].reshape(bs, B_HEADS * HEAD_DIM),
            yc[:, :C_HEADS].reshape(bs, C_HEADS * HEAD_DIM), sn)


def _outproj_body(ya_ref, ob_ref, yc_ref, wa_ref, wb_ref, wc_ref, bias_ref, h_ref, y_ref):
    acc = jnp.dot(ya_ref[...].astype(BF16), wa_ref[...], preferred_element_type=F32)
    acc = acc + jnp.dot(ob_ref[...].astype(BF16), wb_ref[...], preferred_element_type=F32)
    acc = acc + jnp.dot(yc_ref[...].astype(BF16), wc_ref[...], preferred_element_type=F32)
    y_ref[...] = DN_ALPHA * h_ref[...] + acc + bias_ref[...]


def _outproj(ya, ob, yc, wa, wb, wc, bias, h, *, tm, tn=1024):
    m = h.shape[0]

    def rowspec(x):
        return pl.BlockSpec((tm, x.shape[1]), lambda j, i: (i, 0))

    def wspec(w):
        return pl.BlockSpec((w.shape[0], tn), lambda j, i: (0, j))

    return pl.pallas_call(
        _outproj_body, grid=(D_MODEL // tn, m // tm),
        in_specs=[rowspec(ya), rowspec(ob), rowspec(yc), wspec(wa), wspec(wb), wspec(wc),
                  pl.BlockSpec((1, tn), lambda j, i: (0, j)), pl.BlockSpec((tm, tn), lambda j, i: (i, j))],
        out_specs=pl.BlockSpec((tm, tn), lambda j, i: (i, j)),
        out_shape=jax.ShapeDtypeStruct((m, D_MODEL), F32),
        compiler_params=_cparams(("parallel", "parallel"), 48), name="outproj",
    )(ya, ob, yc, wa, wb, wc, bias, h)


def _ln_router_body(y_ref, g_ref, b_ref, rw_ref, rb_ref, h_ref, lg_ref, *, n_real):
    i = pl.program_id(0)

    @pl.when(i < n_real)
    def _():
        h = _ln_rows(y_ref[...], g_ref[...], b_ref[...])
        h_ref[...] = h
        lg_ref[...] = jnp.dot(h, rw_ref[...], precision=HIGHEST, preferred_element_type=F32) + rb_ref[...]

    @pl.when(i >= n_real)
    def _():
        h_ref[...] = jnp.zeros_like(h_ref)
        lg_ref[...] = jnp.zeros_like(lg_ref)


def _ln_router(y, g, b, rw_pad, rb_pad, tm, extra_blocks=0):
    m, d = y.shape
    n_real = m // tm
    n_out = n_real + extra_blocks
    row_in = pl.BlockSpec((tm, d), lambda i: (jnp.minimum(i, n_real - 1), 0))
    vec = pl.BlockSpec((1, d), lambda i: (0, 0))
    return pl.pallas_call(
        functools.partial(_ln_router_body, n_real=n_real), grid=(n_out,),
        in_specs=[row_in, vec, vec, pl.BlockSpec((d, LANES), lambda i: (0, 0)), pl.BlockSpec((1, LANES), lambda i: (0, 0))],
        out_specs=[pl.BlockSpec((tm, d), lambda i: (i, 0)), pl.BlockSpec((tm, LANES), lambda i: (i, 0))],
        out_shape=[jax.ShapeDtypeStruct((n_out * tm, d), F32), jax.ShapeDtypeStruct((n_out * tm, LANES), F32)],
        compiler_params=_cparams(("arbitrary",), 48), name="ln_router",
    )(y, g, b, rw_pad, rb_pad)


def _moe_gather_body(src_ref, na_ref, h_hbm, xs_ref, buf, sem, *, tm):
    i = pl.program_id(0)
    na = na_ref[0]

    def row_copy(blk, slot, r):
        return pltpu.make_async_copy(h_hbm.at[pl.ds(src_ref[blk * tm + r], 1), :], buf.at[slot, pl.ds(r, 1), :],
                                     sem.at[slot])

    def issue(blk, slot):
        def body(r2, carry):
            row_copy(blk, slot, 2 * r2).start(priority=0)
            row_copy(blk, slot, 2 * r2 + 1).start(priority=1)
            return carry
        lax.fori_loop(0, tm // 2, body, 0)

    @pl.when(i == 0)
    def _():
        issue(0, 0)

    slot = i % 2

    @pl.when(i + 1 < na)
    def _():
        issue(i + 1, 1 - slot)

    @pl.when(i < na)
    def _():
        def wait_body(r, carry):
            row_copy(i, slot, r).wait()
            return carry
        lax.fori_loop(0, tm, wait_body, 0)
        xs_ref[...] = buf[slot].astype(BF16)

    @pl.when(i >= na)
    def _():
        xs_ref[...] = jnp.zeros_like(xs_ref)


def _moe_gather(h_all, src, n_active, tm):
    d = h_all.shape[1]
    n_blk = src.shape[0] // tm
    assert tm % 2 == 0
    return pl.pallas_call(
        functools.partial(_moe_gather_body, tm=tm),
        grid_spec=pltpu.PrefetchScalarGridSpec(
            num_scalar_prefetch=2, grid=(n_blk,), in_specs=[pl.BlockSpec(memory_space=pl.ANY)],
            out_specs=pl.BlockSpec((tm, d), lambda i, sr, na: (i, 0)),
            scratch_shapes=[pltpu.VMEM((2, tm, d), F32), pltpu.SemaphoreType.DMA((2,))]),
        out_shape=jax.ShapeDtypeStruct((n_blk * tm, d), BF16),
        compiler_params=_cparams(("arbitrary",), 40), name="moe_gather",
    )(src, n_active, h_all)


def _weight_stream(w_hbms, stages, sem, be_ref, nb_ref, na_ref, cnt_ref, *, layer, tn, on_ready):
    j = pl.program_id(0)
    i = pl.program_id(1)
    n_j = pl.num_programs(0)
    na = na_ref[0]
    active = i < na
    fresh = active & ((i == 0) | (be_ref[i] != be_ref[jnp.maximum(i - 1, 0)]))

    def copies(e, jj, slot):
        cols = pl.ds(pl.multiple_of(jj * tn, tn), tn)
        return [pltpu.make_async_copy(w.at[layer, e, :, cols], st.at[slot], sem.at[t, slot])
                for t, (w, st) in enumerate(zip(w_hbms, stages))]

    @pl.when((j == 0) & (i == 0))
    def _():
        cnt_ref[0] = 0
        for cp in copies(be_ref[0], 0, 0):
            cp.start()

    @pl.when(fresh)
    def _():
        slot = cnt_ref[0] % 2
        for cp in copies(be_ref[i], j, slot):
            cp.wait()
        nxt = nb_ref[i]
        same_j = nxt < na
        e_next = be_ref[jnp.where(same_j, nxt, 0)]
        j_next = jnp.where(same_j, j, j + 1)

        @pl.when(same_j | (j + 1 < n_j))
        def _():
            for cp in copies(e_next, j_next, 1 - slot):
                cp.start()

        on_ready(slot)
        cnt_ref[0] = cnt_ref[0] + 1

    return active


def _moe_up_body(be_ref, nb_ref, na_ref, x_ref, bg_ref, bu_ref, wg_hbm, wu_hbm, a_ref,
                 stage_g, stage_u, wg_bf, wu_bf, sem, cnt_ref, *, layer, tn):
    def on_ready(slot):
        wg_bf[...] = stage_g[slot].astype(BF16)
        wu_bf[...] = stage_u[slot].astype(BF16)

    active = _weight_stream((wg_hbm, wu_hbm), (stage_g, stage_u), sem, be_ref, nb_ref, na_ref, cnt_ref,
                            layer=layer, tn=tn, on_ready=on_ready)

    @pl.when(active)
    def _():
        x = x_ref[...]
        g = jnp.dot(x, wg_bf[...], preferred_element_type=F32) + bg_ref[...]
        u = jnp.dot(x, wu_bf[...], preferred_element_type=F32) + bu_ref[...]
        g = jnp.minimum(g, SWIGLU_LIMIT)
        u = jnp.clip(u, -SWIGLU_LIMIT, SWIGLU_LIMIT)
        a_ref[...] = (g * jax.nn.sigmoid(SWIGLU_ALPHA * g) * (u + 1.0)).astype(BF16)

    @pl.when(jnp.logical_not(active))
    def _():
        a_ref[...] = jnp.zeros_like(a_ref)


def _moe_down_body(be_ref, nb_ref, na_ref, a_ref, bd_ref, wd_hbm, y_ref, stage_d, wd_bf, sem, cnt_ref, *, layer, tn):
    def on_ready(slot):
        wd_bf[...] = stage_d[slot].astype(BF16)

    active = _weight_stream((wd_hbm,), (stage_d,), sem, be_ref, nb_ref, na_ref, cnt_ref,
                            layer=layer, tn=tn, on_ready=on_ready)

    @pl.when(active)
    def _():
        y_ref[...] = jnp.dot(a_ref[...], wd_bf[...], preferred_element_type=F32) + bd_ref[...]

    @pl.when(jnp.logical_not(active))
    def _():
        y_ref[...] = jnp.zeros_like(y_ref)


def _moe_block_rows(n_rows):
    mean = n_rows / N_EXPERTS
    return int(-(-(mean + 2.0 * mean ** 0.5) // (3 * 2 * SUBLANES)) * 2 * SUBLANES)


def _moe_experts(layer, xs, block_expert, next_block, n_active, w_gate, b_gate, w_up, b_up, w_down, b_down, tm):
    m_pad, d = xs.shape
    d_ff = w_gate.shape[-1]
    n_blk = m_pad // tm

    def blk(i, na):
        return jnp.minimum(i, na[0] - 1)

    def rows(width):
        return pl.BlockSpec((tm, width), lambda j, i, be, nb, na: (blk(i, na), 0))

    def bias(t):
        return pl.BlockSpec((None, None, 1, t), lambda j, i, be, nb, na: (layer, be[i], 0, j))

    def out(t):
        return pl.BlockSpec((tm, t), lambda j, i, be, nb, na: (i, j))

    hbm = pl.BlockSpec(memory_space=pl.ANY)
    smem_cnt = pltpu.SMEM((1,), jnp.int32)

    tn = MOE_TN_UP
    act = pl.pallas_call(
        functools.partial(_moe_up_body, layer=layer, tn=tn),
        grid_spec=pltpu.PrefetchScalarGridSpec(
            num_scalar_prefetch=3, grid=(d_ff // tn, n_blk),
            in_specs=[rows(d), bias(tn), bias(tn), hbm, hbm],
            out_specs=out(tn),
            scratch_shapes=[pltpu.VMEM((2, d, tn), F32), pltpu.VMEM((2, d, tn), F32),
                            pltpu.VMEM((d, tn), BF16), pltpu.VMEM((d, tn), BF16),
                            pltpu.SemaphoreType.DMA((2, 2)), smem_cnt]),
        out_shape=jax.ShapeDtypeStruct((m_pad, d_ff), BF16),
        compiler_params=_cparams(("arbitrary", "arbitrary"), 56), name="moe_gate_up",
    )(block_expert, next_block, n_active, xs,
      b_gate.reshape(DEPTH, N_EXPERTS, 1, d_ff), b_up.reshape(DEPTH, N_EXPERTS, 1, d_ff), w_gate, w_up)

    tn2 = MOE_TN_DOWN
    return pl.pallas_call(
        functools.partial(_moe_down_body, layer=layer, tn=tn2),
        grid_spec=pltpu.PrefetchScalarGridSpec(
            num_scalar_prefetch=3, grid=(d // tn2, n_blk),
            in_specs=[rows(d_ff), bias(tn2), hbm],
            out_specs=out(tn2),
            scratch_shapes=[pltpu.VMEM((2, d_ff, tn2), F32), pltpu.VMEM((d_ff, tn2), BF16),
                            pltpu.SemaphoreType.DMA((1, 2)), smem_cnt]),
        out_shape=jax.ShapeDtypeStruct((m_pad, d), F32),
        compiler_params=_cparams(("arbitrary", "arbitrary"), 56), name="moe_down",
    )(block_expert, next_block, n_active, act, b_down.reshape(DEPTH, N_EXPERTS, 1, d), w_down)


def _combine_body(dest_ref, h_ref, gt_ref, g_ref, b_ref, yb_hbm, o_ref, ob_ref, yg, sem, *, tm):
    t = pl.program_id(0)
    n_tiles = pl.num_programs(0)

    def row_copy(tile, slot, r, kk):
        d = dest_ref[(tile * tm + r) * TOP_K + kk]
        return pltpu.make_async_copy(yb_hbm.at[pl.ds(d, 1), :], yg.at[slot, kk, pl.ds(r, 1), :], sem.at[slot])

    def issue(tile, slot):
        def body(r, carry):
            for kk in range(TOP_K):
                row_copy(tile, slot, r, kk).start(priority=kk % 2)
            return carry
        lax.fori_loop(0, tm, body, 0)

    @pl.when(t == 0)
    def _():
        issue(0, 0)

    slot = t % 2

    @pl.when(t + 1 < n_tiles)
    def _():
        issue(t + 1, 1 - slot)

    def wait_body(r, carry):
        for kk in range(TOP_K):
            row_copy(t, slot, r, kk).wait()
        return carry
    lax.fori_loop(0, tm, wait_body, 0)

    gt = gt_ref[...]
    moe = ((yg[slot, 0] * gt[:, 0:1] + yg[slot, 1] * gt[:, 1:2])
           + (yg[slot, 2] * gt[:, 2:3] + yg[slot, 3] * gt[:, 3:4]))
    y = _ln_rows(DN_ALPHA * h_ref[...] + moe, g_ref[...], b_ref[...])
    o_ref[...] = y
    ob_ref[...] = y.astype(BF16)


def _combine_ln(h1, yb, dest, gates, g, b, tm, m):
    d = h1.shape[1]
    row = pl.BlockSpec((tm, d), lambda i, dr: (i, 0))
    vec = pl.BlockSpec((1, d), lambda i, dr: (0, 0))
    return pl.pallas_call(
        functools.partial(_combine_body, tm=tm),
        grid_spec=pltpu.PrefetchScalarGridSpec(
            num_scalar_prefetch=1, grid=(m // tm,),
            in_specs=[row, pl.BlockSpec((tm, TOP_K), lambda i, dr: (i, 0)), vec, vec,
                      pl.BlockSpec(memory_space=pl.ANY)],
            out_specs=[row, row],
            scratch_shapes=[pltpu.VMEM((2, TOP_K, tm, d), F32), pltpu.SemaphoreType.DMA((2,))]),
        out_shape=[jax.ShapeDtypeStruct((m, d), F32), jax.ShapeDtypeStruct((m, d), BF16)],
        compiler_params=_cparams(("arbitrary",), 48), name="combine_ln2",
    )(dest, h1, gates, g, b, yb)


def _route(logits, tm):
    n_tok = logits.shape[0]
    top_v, top_e = lax.top_k(logits, TOP_K)
    gates = jax.nn.softmax(top_v, axis=-1)
    m = n_tok * TOP_K
    e_flat = top_e.reshape(m)
    n_blk = -(-m // tm) + N_EXPERTS
    onehot = jax.nn.one_hot(e_flat, N_EXPERTS, dtype=jnp.int32)
    counts = jnp.sum(onehot, axis=0)
    first = jnp.cumsum(counts) - counts
    order = jnp.argsort(e_flat).astype(jnp.int32)
    pos = jnp.argsort(order).astype(jnp.int32)
    blocks_per = (counts + tm - 1) // tm
    blocks_end = jnp.cumsum(blocks_per)
    row0 = (blocks_end - blocks_per) * tm
    dest = pos + jnp.sum(onehot * (row0 - first)[None, :], axis=1)
    n_active = blocks_end[-1]
    bidx = jnp.minimum(jnp.arange(n_blk), n_active - 1)
    block_expert = jnp.minimum(jnp.sum(bidx[:, None] >= blocks_end[None, :], axis=1), N_EXPERTS - 1).astype(jnp.int32)
    next_block = blocks_end[block_expert].astype(jnp.int32)
    off = (jnp.arange(n_blk)[:, None] * tm - row0[block_expert][:, None]) + jnp.arange(tm)[None, :]
    valid = (off < counts[block_expert][:, None]) & (jnp.arange(n_blk)[:, None] < n_active)
    sorted_pos = jnp.clip(first[block_expert][:, None] + off, 0, m - 1)
    src = jnp.where(valid, order[sorted_pos] // TOP_K, 0).reshape(n_blk * tm)
    return gates, dest.astype(jnp.int32), src, block_expert, next_block, n_active.reshape(1).astype(jnp.int32)


def _rope_tables(pos):
    half = HEAD_DIM // 2
    inv_freq = ROPE_THETA ** (-np.arange(half, dtype=np.float64) / half)
    ang = np.asarray(pos, np.float64)[:, None] * inv_freq[None, :]
    cos, sin = np.cos(ang), np.sin(ang)
    return (jnp.asarray(np.concatenate([cos, cos], axis=1), F32),
            jnp.asarray(np.concatenate([-sin, sin], axis=1), F32))


def _split_w_in(w, b):
    k = w.shape[0]
    w = w.astype(BF16)
    w_r = jnp.concatenate([w[:, _O_QB:_O_VB], w[:, _O_QC:_O_VC], jnp.zeros((k, LANES), w.dtype)], axis=1)
    w_p = jnp.concatenate([w[:, _O_QA:_O_LR], w[:, _O_VB:_O_QC], w[:, _O_VC:_O_END], w[:, _O_LR:_O_QB],
                           jnp.zeros((k, LANES - A_LOWRANK), w.dtype)], axis=1)
    b_r = jnp.concatenate([b[_O_QB:_O_VB], b[_O_QC:_O_VC], jnp.zeros((LANES,), b.dtype)])
    b_p = jnp.concatenate([b[_O_QA:_O_LR], b[_O_VB:_O_QC], b[_O_VC:_O_END], b[_O_LR:_O_QB],
                           jnp.zeros((LANES - A_LOWRANK,), b.dtype)])
    return w_r, w_p, b_r.reshape(1, W_PROJ), b_p.reshape(1, W_PROJ)


def _mix_prompt(rp, pp, wlr_pad, blr, ng, sink, bp, tlen):
    ya, st = _gla_prompt(pp, wlr_pad, blr, ng, bp, tlen)
    ob = _swa_attention(rp, pp, sink, bp, tlen)
    yc = _dilated_attention(rp, pp, bp, tlen)
    return ya, ob, yc, st


def kernel(x_prompt, x_sample, cache_b_kv, cache_c1_kv, cache_c2_kv, cache_c3_kv, state_gla, ln_in_g, ln_in_b, w_in, b_in, gla_w_gate, gla_b_gate, gla_norm_g, attn_sinks, w_out, b_out, ln1_g, ln1_b, router_w, router_b, w_gate, b_gate, w_up, b_up, w_down, b_down, ln2_g, ln2_b):
    bp, tlen, d = x_prompt.shape
    bs, slen, _ = x_sample.shape
    assert slen == 1 and d == D_MODEL and tlen % SWA_ROWS == 0
    n_p, n_s = bp * tlen, bs * slen
    tm_p = 256

    cos_p, sin_p = _rope_tables(np.tile(np.arange(tlen), bp))
    cos_s, sin_s = _rope_tables(np.full((n_s,), PAST_LEN))
    caches_c = (cache_c1_kv, cache_c2_kv, cache_c3_kv)

    hp, hp_b = _ln_in(x_prompt.reshape(n_p, d), ln_in_g, ln_in_b, tm_p)
    hs, hs_b = _ln_in(x_sample.reshape(n_s, d), ln_in_g, ln_in_b, n_s)

    new_p = [[] for _ in range(5)]
    new_s = [[] for _ in range(5)]
    for l in range(DEPTH):
        w_r, w_p, b_r, b_p = _split_w_in(w_in[l], b_in[l])
        wlr_pad = jnp.concatenate([gla_w_gate[l], jnp.zeros((LANES - A_LOWRANK, A_HEADS * A_DK), F32)], axis=0)
        blr = gla_b_gate[l].reshape(1, A_HEADS * A_DK)
        ng = gla_norm_g[l].reshape(1, A_DV)
        sink = attn_sinks[l].reshape(B_HEADS)
        wo = w_out[l].astype(BF16)
        na, nb = A_HEADS * A_DV, B_HEADS * HEAD_DIM
        wo_a, wo_b, wo_c = wo[:na], wo[na:na + nb], wo[na + nb:]
        bo = b_out[l].reshape(1, d)

        rp = _proj(hp_b, w_r, b_r, cos_p, sin_p, tm=512)
        pp = _proj(hp_b, w_p, b_p, tm=512)
        rs = _proj(hs_b, w_r, b_r, cos_s, sin_s, tm=n_s, tn=W_PROJ // 2)
        ps = _proj(hs_b, w_p, b_p, tm=n_s, tn=W_PROJ // 2)

        ya_p, ob_p, yc_p, st_p = _mix_prompt(rp, pp, wlr_pad, blr, ng, sink, bp, tlen)
        ya_s, ob_s, yc_s, st_s = _sample_mixers(l, rs, ps, state_gla, cache_b_kv, caches_c, sink, wlr_pad, blr, ng)

        rp4 = rp.reshape(bp, tlen, W_PROJ)
        pp4 = pp.reshape(bp, tlen, W_PROJ)

        def kv_prompt(uk, uv, heads, rows):
            kk = rp4[:, tlen - rows:, uk * LANES:(uk + heads) * LANES].reshape(bp, rows, heads, HEAD_DIM)
            vv = pp4[:, tlen - rows:, uv * LANES:(uv + heads) * LANES].reshape(bp, rows, heads, HEAD_DIM)
            return jnp.stack([kk, vv], axis=1)

        def kv_sample(uk, uv, heads):
            kk = rs[:, uk * LANES:(uk + heads) * LANES].reshape(bs, slen, heads, HEAD_DIM)
            vv = ps[:, uv * LANES:(uv + heads) * LANES].reshape(bs, slen, heads, HEAD_DIM)
            return jnp.stack([kk, vv], axis=1)

        new_p[0].append(kv_prompt(R_KB, P_VB, B_KV_HEADS, min(B_WINDOW, tlen)))
        new_s[0].append(kv_sample(R_KB, P_VB, B_KV_HEADS))
        for g, (win, dil) in enumerate(C_PATTERNS):
            new_p[1 + g].append(kv_prompt(R_KC + C_HEADS * g, P_VC + C_HEADS * g, C_HEADS, min(win, tlen)))
            new_s[1 + g].append(kv_sample(R_KC + C_HEADS * g, P_VC + C_HEADS * g, C_HEADS))
        new_p[4].append(st_p)
        new_s[4].append(st_s)

        y_p = _outproj(ya_p, ob_p, yc_p, wo_a, wo_b, wo_c, bo, hp, tm=512)
        y_s = _outproj(ya_s, ob_s, yc_s, wo_a, wo_b, wo_c, bo, hs, tm=n_s, tn=D_MODEL // 2)
        rw_pad = jnp.concatenate([router_w[l], jnp.zeros((d, LANES - N_EXPERTS), F32)], axis=1)
        rb_pad = jnp.concatenate([router_b[l], jnp.zeros((LANES - N_EXPERTS,), F32)]).reshape(1, LANES)
        g1, b1 = ln1_g[l].reshape(1, d), ln1_b[l].reshape(1, d)
        h1_p, lg_p = _ln_router(y_p, g1, b1, rw_pad, rb_pad, tm_p, extra_blocks=1)
        h1_s, lg_s = _ln_router(y_s, g1, b1, rw_pad, rb_pad, n_s)
        h1_all = lax.dynamic_update_slice(h1_p, h1_s, (n_p, 0))

        logits = jnp.concatenate([lg_p[:n_p], lg_s], axis=0)[:, :N_EXPERTS]
        moe_tm = _moe_block_rows((n_p + n_s) * TOP_K)
        gates, dest, src, block_expert, next_block, n_active = _route(logits, moe_tm)
        xs = _moe_gather(h1_all, src, n_active, moe_tm)
        yb = _moe_experts(l, xs, block_expert, next_block, n_active, w_gate, b_gate, w_up, b_up, w_down, b_down, moe_tm)
        g2, b2 = ln2_g[l].reshape(1, d), ln2_b[l].reshape(1, d)
        hp, hp_b = _combine_ln(h1_all, yb, dest[:n_p * TOP_K], gates[:n_p], g2, b2, 128, n_p)
        hs, hs_b = _combine_ln(h1_s, yb, dest[n_p * TOP_K:], gates[n_p:], g2, b2, n_s, n_s)

    outs = [hp.reshape(bp, tlen, d), hs.reshape(bs, slen, d)]
    for i in range(5):
        outs += [jnp.stack(new_p[i], axis=1), jnp.stack(new_s[i], axis=1)]
    return tuple(outs)
```

```python
import functools

import numpy as np
import jax
import jax.numpy as jnp
from jax import lax
from jax.experimental import pallas as pl
from jax.experimental.pallas import tpu as pltpu

F32 = jnp.float32
BF16 = jnp.bfloat16
HIGHEST = lax.Precision.HIGHEST

LANES = 128
SUBLANES = 8

D_MODEL = 4096
DEPTH = 2
PAST_LEN = 16384
HEAD_DIM = 128
A_HEADS, A_DK, A_DV, A_LOWRANK = 4, 128, 256, 16
A_GATE_NORM = 16.0
A_CHUNK = 64
B_HEADS, B_KV_HEADS = 12, 3
B_GROUP = B_HEADS // B_KV_HEADS
B_WINDOW = 128
C_PATTERNS = ((128, 1), (512, 4), (2048, 16))
N_C_GROUPS = len(C_PATTERNS)
C_HEADS = 4
ATTN_BLOCK = 128
ROPE_THETA = 10000.0
N_EXPERTS = 32
TOP_K = 4
SWIGLU_LIMIT = 7.0
SWIGLU_ALPHA = 1.702
DN_ALPHA = (2 * DEPTH) ** 0.25
LN_EPS = 1e-5
RMS_EPS = 1e-6
NEG_INF = -1e30
ATTN_SCALE = HEAD_DIM ** -0.5

_O_QA, _O_KA, _O_VA, _O_GA, _O_LR, _O_QB, _O_KB, _O_VB, _O_QC, _O_KC, _O_VC, _O_END = (
    0, 512, 1024, 2048, 3072, 3088, 4624, 5008, 5392, 6928, 8464, 10000)
R_QB, R_KB, R_QC, R_KC = 0, 12, 15, 27
P_QA, P_KA, P_VA, P_GA, P_VB, P_VC, P_LR = 0, 4, 8, 16, 24, 27, 39
U_PROJ = 40
W_PROJ = U_PROJ * LANES

S_ROWS = 16 + SUBLANES * N_C_GROUPS

MOE_TN_UP = 512
MOE_TN_DOWN = 1024
SWA_ROWS = 512


def _cparams(sem, vmem_mb):
    return pltpu.CompilerParams(dimension_semantics=sem, vmem_limit_bytes=vmem_mb * 1024 * 1024)


def _ln_rows(x, g, b):
    mu = jnp.mean(x, axis=-1, keepdims=True)
    xc = x - mu
    var = jnp.mean(xc * xc, axis=-1, keepdims=True)
    return xc * lax.rsqrt(var + LN_EPS) * g + b


def _ln_in_body(x_ref, g_ref, b_ref, o_ref, ob_ref):
    y = _ln_rows(x_ref[...], g_ref[...], b_ref[...])
    o_ref[...] = y
    ob_ref[...] = y.astype(BF16)


def _ln_in(x, g, b, tm):
    m, d = x.shape
    row = pl.BlockSpec((tm, d), lambda i: (i, 0))
    vec = pl.BlockSpec((1, d), lambda i: (0, 0))
    return pl.pallas_call(
        _ln_in_body, grid=(m // tm,), in_specs=[row, vec, vec], out_specs=[row, row],
        out_shape=[jax.ShapeDtypeStruct((m, d), F32), jax.ShapeDtypeStruct((m, d), BF16)],
        compiler_params=_cparams(("parallel",), 48), name="ln_in",
    )(x, g.reshape(1, d), b.reshape(1, d))


def _proj_body(*refs, rope, tn):
    if rope:
        x_ref, w_ref, b_ref, cos_ref, sin_ref, o_ref = refs
    else:
        x_ref, w_ref, b_ref, o_ref = refs
    acc = jnp.dot(x_ref[...], w_ref[...], preferred_element_type=F32) + b_ref[...]
    if not rope:
        o_ref[...] = acc
        return
    cos = cos_ref[...]
    sin = sin_ref[...]
    for u in range(tn // LANES):
        seg = acc[:, u * LANES:(u + 1) * LANES]
        o_ref[:, u * LANES:(u + 1) * LANES] = seg * cos + pltpu.roll(seg, HEAD_DIM // 2, 1) * sin


def _proj(xb, w, b, cos=None, sin=None, *, tm, tn=1024):
    m, k = xb.shape
    n = w.shape[1]
    rope = cos is not None
    in_specs = [pl.BlockSpec((tm, k), lambda j, i: (i, 0)),
                pl.BlockSpec((k, tn), lambda j, i: (0, j)),
                pl.BlockSpec((1, tn), lambda j, i: (0, j))]
    args = [xb, w, b]
    if rope:
        tab = pl.BlockSpec((tm, LANES), lambda j, i: (i, 0))
        in_specs += [tab, tab]
        args += [cos, sin]
    return pl.pallas_call(
        functools.partial(_proj_body, rope=rope, tn=tn), grid=(n // tn, m // tm),
        in_specs=in_specs, out_specs=pl.BlockSpec((tm, tn), lambda j, i: (i, j)),
        out_shape=jax.ShapeDtypeStruct((m, n), F32),
        compiler_params=_cparams(("parallel", "parallel"), 48),
        name="proj_rope" if rope else "proj_plain",
    )(*args)


def _log_sigmoid(z):
    return jnp.minimum(z, 0.0) - jnp.log(1.0 + jnp.exp(-jnp.abs(z)))


def _rms_gate(o, ng, ga):
    o = o * lax.rsqrt(jnp.mean(o * o, axis=-1, keepdims=True) + RMS_EPS) * ng
    return o * (ga * jax.nn.sigmoid(ga))


def _gla_body(q_ref, k_ref, v_ref, ga_ref, lr_ref, wlr_ref, blr_ref, ng_ref, ya_ref, st_ref, s_scr, *, n_sub):
    c = pl.program_id(2)

    @pl.when(c == 0)
    def _():
        s_scr[...] = jnp.zeros_like(s_scr)

    ch = A_CHUNK
    rr = lax.broadcasted_iota(jnp.int32, (ch, ch), 0)
    cc = lax.broadcasted_iota(jnp.int32, (ch, ch), 1)
    causal = rr >= cc
    tri = causal.astype(F32)
    ones = jnp.ones((ch, A_DK), F32)
    for u in range(n_sub):
        sl = pl.ds(u * ch, ch)
        z = jnp.dot(lr_ref[sl, :], wlr_ref[...], precision=HIGHEST, preferred_element_type=F32) + blr_ref[...]
        g = _log_sigmoid(z) / A_GATE_NORM
        b = jnp.dot(tri, g, precision=HIGHEST, preferred_element_type=F32)
        b_last = b[ch - 1:ch, :]
        b_last_col = lax.dot_general(g, ones, (((0,), (0,)), ((), ())), precision=HIGHEST,
                                     preferred_element_type=F32)
        q = q_ref[sl, :] * (A_DK ** -0.5)
        k = k_ref[sl, :]
        v = v_ref[sl, :].astype(BF16)
        qg = (q * jnp.exp(b)).astype(BF16)
        kg = (k * jnp.exp(-b)).astype(BF16)
        kd = (k * jnp.exp(b_last - b)).astype(BF16)
        att = lax.dot_general(qg, kg, (((1,), (1,)), ((), ())), preferred_element_type=F32)
        att = jnp.where(causal, att, 0.0).astype(BF16)
        s = s_scr[...]
        o = (jnp.dot(qg, s.astype(BF16), preferred_element_type=F32)
             + jnp.dot(att, v, preferred_element_type=F32))
        decay = jnp.exp(b_last_col)
        decay = jnp.concatenate([decay, decay], axis=1)
        s_scr[...] = decay * s + lax.dot_general(kd, v, (((0,), (0,)), ((), ())), preferred_element_type=F32)
        ya_ref[sl, :] = _rms_gate(o, ng_ref[...], ga_ref[sl, :])

    @pl.when(c == pl.num_programs(2) - 1)
    def _():
        st_ref[...] = s_scr[...]


def _gla_prompt(p_arr, wlr_pad, blr, ng, bsz, tlen, rb=256):
    nblk = tlen // rb
    n_sub = rb // A_CHUNK

    def rows(b, c):
        return b * nblk + c

    in_specs = [
        pl.BlockSpec((rb, A_DK), lambda b, h, c: (rows(b, c), P_QA + h)),
        pl.BlockSpec((rb, A_DK), lambda b, h, c: (rows(b, c), P_KA + h)),
        pl.BlockSpec((rb, A_DV), lambda b, h, c: (rows(b, c), P_VA // 2 + h)),
        pl.BlockSpec((rb, A_DV), lambda b, h, c: (rows(b, c), P_GA // 2 + h)),
        pl.BlockSpec((rb, LANES), lambda b, h, c: (rows(b, c), P_LR)),
        pl.BlockSpec((LANES, A_DK), lambda b, h, c: (0, h)),
        pl.BlockSpec((1, A_DK), lambda b, h, c: (0, h)),
        pl.BlockSpec((1, A_DV), lambda b, h, c: (0, 0)),
    ]
    out_specs = [
        pl.BlockSpec((rb, A_DV), lambda b, h, c: (rows(b, c), h)),
        pl.BlockSpec((None, None, A_DK, A_DV), lambda b, h, c: (b, h, 0, 0)),
    ]
    return pl.pallas_call(
        functools.partial(_gla_body, n_sub=n_sub), grid=(bsz, A_HEADS, nblk),
        in_specs=in_specs, out_specs=out_specs,
        out_shape=[jax.ShapeDtypeStruct((bsz * tlen, A_HEADS * A_DV), F32),
                   jax.ShapeDtypeStruct((bsz, A_HEADS, A_DK, A_DV), F32)],
        scratch_shapes=[pltpu.VMEM((A_DK, A_DV), F32)],
        compiler_params=_cparams(("parallel", "parallel", "arbitrary"), 32), name="gla_prompt",
    )(p_arr, p_arr, p_arr, p_arr, p_arr, wlr_pad, blr, ng)


def _attn_block(q16, k_prev, k_cur, v_prev, v_cur, prev_on, sink):
    blk = ATTN_BLOCK
    rows = q16.shape[0]
    nk = blk if k_prev is None else 2 * blk
    r = lax.broadcasted_iota(jnp.int32, (rows, nk), 0)
    if rows > blk:
        r = r % blk
    c = lax.broadcasted_iota(jnp.int32, (rows, nk), 1)
    if k_prev is None:
        keys, vals = k_cur.astype(BF16), v_cur.astype(BF16)
        mask = c <= r
    else:
        keys = jnp.concatenate([k_prev, k_cur], axis=0).astype(BF16)
        vals = jnp.concatenate([v_prev, v_cur], axis=0).astype(BF16)
        in_prev = (c < blk) & (c >= r)
        if prev_on is not True:
            in_prev = in_prev & prev_on
        mask = in_prev | ((c >= blk) & (c - blk <= r))
    s = lax.dot_general(q16, keys, (((1,), (1,)), ((), ())), preferred_element_type=F32) * ATTN_SCALE
    s = jnp.where(mask, s, NEG_INF)
    m = jnp.max(s, axis=1, keepdims=True)
    if sink is not None:
        m = jnp.maximum(m, sink)
    p = jnp.exp(s - m)
    den = jnp.sum(p, axis=1, keepdims=True)
    if sink is not None:
        den = den + jnp.exp(sink - m)
    o = jnp.dot(p.astype(BF16), vals, preferred_element_type=F32) / den
    return o, m + jnp.log(den)


def _swa_body(sink_ref, q_ref, kc_ref, vc_ref, kp_ref, vp_ref, o_ref):
    blk = ATTN_BLOCK
    hk = pl.program_id(1)
    n = pl.program_id(2)
    sink = jnp.concatenate([jnp.full((blk, 1), sink_ref[hk * B_GROUP + g], F32) for g in range(B_GROUP)], axis=0)
    for i in range(SWA_ROWS // blk):
        rows = slice(i * blk, (i + 1) * blk)
        q = q_ref[rows, :]
        q16 = jnp.concatenate([q[:, g * HEAD_DIM:(g + 1) * HEAD_DIM] for g in range(B_GROUP)], axis=0).astype(BF16)
        if i == 0:
            k_prev, v_prev, prev_on = kp_ref[...], vp_ref[...], n > 0
        else:
            prev = slice((i - 1) * blk, i * blk)
            k_prev, v_prev, prev_on = kc_ref[prev, :], vc_ref[prev, :], True
        o, _ = _attn_block(q16, k_prev, kc_ref[rows, :], v_prev, vc_ref[rows, :], prev_on, sink)
        o_ref[rows, :] = jnp.concatenate([o[g * blk:(g + 1) * blk, :] for g in range(B_GROUP)], axis=1)


def _swa_attention(r_arr, p_arr, sink, bsz, tlen):
    nsb = tlen // SWA_ROWS
    sub = SWA_ROWS // ATTN_BLOCK
    gw = B_GROUP * HEAD_DIM

    def prev_blk(b, n):
        return (b * nsb + n) * sub - jnp.minimum(n, 1)

    in_specs = [
        pl.BlockSpec(memory_space=pltpu.SMEM),
        pl.BlockSpec((SWA_ROWS, gw), lambda b, h, n: (b * nsb + n, R_QB // B_GROUP + h)),
        pl.BlockSpec((SWA_ROWS, HEAD_DIM), lambda b, h, n: (b * nsb + n, R_KB + h)),
        pl.BlockSpec((SWA_ROWS, HEAD_DIM), lambda b, h, n: (b * nsb + n, P_VB + h)),
        pl.BlockSpec((ATTN_BLOCK, HEAD_DIM), lambda b, h, n: (prev_blk(b, n), R_KB + h)),
        pl.BlockSpec((ATTN_BLOCK, HEAD_DIM), lambda b, h, n: (prev_blk(b, n), P_VB + h)),
    ]
    return pl.pallas_call(
        _swa_body, grid=(bsz, B_KV_HEADS, nsb), in_specs=in_specs,
        out_specs=pl.BlockSpec((SWA_ROWS, gw), lambda b, h, n: (b * nsb + n, h)),
        out_shape=jax.ShapeDtypeStruct((bsz * tlen, B_HEADS * HEAD_DIM), F32),
        compiler_params=_cparams(("parallel", "parallel", "parallel"), 32), name="swa_attn",
    )(sink, r_arr, r_arr, p_arr, r_arr, p_arr)


def _dilated_body(*refs, tlen):
    qkv = refs[:3 * N_C_GROUPS]
    yc_ref = refs[3 * N_C_GROUPS]
    o_scr = refs[3 * N_C_GROUPS + 1:3 * N_C_GROUPS + 1 + N_C_GROUPS]
    l_scr = refs[3 * N_C_GROUPS + 1 + N_C_GROUPS:]
    blk = ATTN_BLOCK
    for g, (win, dil) in enumerate(C_PATTERNS):
        q_ref, k_ref, v_ref = qkv[3 * g:3 * g + 3]
        for res in range(dil):
            for i in range(tlen // dil // blk):
                def rows(ii):
                    start = res + dil * blk * ii
                    return pl.ds(start, blk) if dil == 1 else pl.ds(start, blk, stride=dil)
                cur = rows(i)
                if i == 0:
                    k_prev = v_prev = None
                else:
                    k_prev, v_prev = k_ref[rows(i - 1), :], v_ref[rows(i - 1), :]
                o, lse = _attn_block(q_ref[cur, :].astype(BF16), k_prev, k_ref[cur, :], v_prev, v_ref[cur, :], True, None)
                o_scr[g][cur, :] = o
                l_scr[g][cur, :] = jnp.broadcast_to(lse, (blk, HEAD_DIM))
    step = 2 * blk
    for ch in range(tlen // step):
        rows = slice(ch * step, (ch + 1) * step)
        ls = [l[rows, :] for l in l_scr]
        lmax = jnp.maximum(jnp.maximum(ls[0], ls[1]), ls[2])
        es = [jnp.exp(l - lmax) for l in ls]
        num = es[0] * o_scr[0][rows, :] + es[1] * o_scr[1][rows, :] + es[2] * o_scr[2][rows, :]
        yc_ref[rows, :] = num / (es[0] + es[1] + es[2])


def _dilated_attention(r_arr, p_arr, bsz, tlen):
    in_specs, args = [], []
    for g, (win, dil) in enumerate(C_PATTERNS):
        assert win // dil == ATTN_BLOCK and tlen % (dil * ATTN_BLOCK) == 0
        in_specs += [pl.BlockSpec((tlen, HEAD_DIM), lambda b, h, g=g: (b, R_QC + C_HEADS * g + h)),
                     pl.BlockSpec((tlen, HEAD_DIM), lambda b, h, g=g: (b, R_KC + C_HEADS * g + h)),
                     pl.BlockSpec((tlen, HEAD_DIM), lambda b, h, g=g: (b, P_VC + C_HEADS * g + h))]
        args += [r_arr, r_arr, p_arr]
    return pl.pallas_call(
        functools.partial(_dilated_body, tlen=tlen), grid=(bsz, C_HEADS), in_specs=in_specs,
        out_specs=pl.BlockSpec((tlen, HEAD_DIM), lambda b, h: (b, h)),
        out_shape=jax.ShapeDtypeStruct((bsz * tlen, C_HEADS * HEAD_DIM), F32),
        scratch_shapes=[pltpu.VMEM((tlen, HEAD_DIM), F32) for _ in range(2 * N_C_GROUPS)],
        compiler_params=_cparams(("parallel", "parallel"), 40), name="dilated_attn",
    )(*args)


def _sample_body(sink_ref, q_ref, kn_ref, vn_ref, qk_ref, va_ref, ga_ref, lr_ref, st_ref,
                 cb_ref, c1_ref, c2_ref, c3_ref, wlr_ref, blr_ref, ng_ref,
                 ya_ref, ob_ref, yc_ref, sn_ref):
    nt = (((1,), (1,)), ((), ()))

    z = jnp.dot(lr_ref[...], wlr_ref[...], precision=HIGHEST, preferred_element_type=F32)[0:1, :] + blr_ref[...]
    a_all = jnp.exp(_log_sigmoid(z) / A_GATE_NORM)
    qk = qk_ref[...]
    rid16 = lax.broadcasted_iota(jnp.int32, (16, 1), 0)
    rows = jnp.zeros((16, A_DK), F32)
    for h in range(A_HEADS):
        a_h = jnp.broadcast_to(a_all[:, h * A_DK:(h + 1) * A_DK], (16, A_DK))
        k_h = jnp.broadcast_to(qk[A_HEADS + h:A_HEADS + h + 1, :], (16, A_DK))
        q_h = jnp.broadcast_to(qk[h:h + 1, :], (16, A_DK)) * (A_DK ** -0.5)
        rows = jnp.where(rid16 == 3 * h, a_h, rows)
        rows = jnp.where(rid16 == 3 * h + 1, k_h, rows)
        rows = jnp.where(rid16 == 3 * h + 2, q_h, rows)
    cols = jnp.concatenate([rows, jnp.zeros((LANES - 16, A_DK), F32)], axis=0).T
    for h in range(A_HEADS):
        a_col, k_col, q_col = cols[:, 3 * h:3 * h + 1], cols[:, 3 * h + 1:3 * h + 2], cols[:, 3 * h + 2:3 * h + 3]
        s_new = a_col * st_ref[h] + k_col * va_ref[h:h + 1, :]
        sn_ref[h] = s_new
        o = jnp.sum(q_col * s_new, axis=0, keepdims=True)
        ya_ref[h:h + 1, :] = _rms_gate(o, ng_ref[...], ga_ref[h:h + 1, :])

    q = q_ref[...]
    q16 = q.astype(BF16)
    rid = lax.broadcasted_iota(jnp.int32, (S_ROWS, 1), 0)

    def pick_b(j):
        return (rid < B_HEADS) & (rid // B_GROUP == j)

    def pick_c(g, hh):
        return rid == 16 + SUBLANES * g + hh

    c_refs = (c1_ref, c2_ref, c3_ref)
    s = jnp.zeros((S_ROWS, ATTN_BLOCK), F32)
    for j in range(B_KV_HEADS):
        kj = cb_ref[0, :, j, :].astype(BF16)
        s = jnp.where(pick_b(j), lax.dot_general(q16, kj, nt, preferred_element_type=F32), s)
    for g in range(N_C_GROUPS):
        for hh in range(C_HEADS):
            kh = c_refs[g][0, :, hh, :].astype(BF16)
            s = jnp.where(pick_c(g, hh), lax.dot_general(q16, kh, nt, preferred_element_type=F32), s)
    s = s * ATTN_SCALE
    s_n = jnp.sum(q * kn_ref[...], axis=1, keepdims=True) * ATTN_SCALE
    sink = jnp.full((S_ROWS, 1), NEG_INF, F32)
    for i in range(B_HEADS):
        sink = jnp.where(rid == i, sink_ref[i], sink)
    m = jnp.maximum(jnp.maximum(jnp.max(s, axis=1, keepdims=True), s_n), sink)
    p = jnp.exp(s - m)
    p_n = jnp.exp(s_n - m)
    den = jnp.sum(p, axis=1, keepdims=True) + p_n + jnp.exp(sink - m)
    p16 = p.astype(BF16)
    acc = p_n * vn_ref[...]
    for j in range(B_KV_HEADS):
        vj = cb_ref[1, :, j, :].astype(BF16)
        acc = acc + jnp.where(pick_b(j), jnp.dot(p16, vj, preferred_element_type=F32), 0.0)
    for g in range(N_C_GROUPS):
        for hh in range(C_HEADS):
            vh = c_refs[g][1, :, hh, :].astype(BF16)
            acc = acc + jnp.where(pick_c(g, hh), jnp.dot(p16, vh, preferred_element_type=F32), 0.0)
    o = acc / den
    lse = m + jnp.log(den)
    ob_ref[...] = o[0:16, :]
    ls = [lse[16 + SUBLANES * g:24 + SUBLANES * g, :] for g in range(N_C_GROUPS)]
    os_ = [o[16 + SUBLANES * g:24 + SUBLANES * g, :] for g in range(N_C_GROUPS)]
    lmax = jnp.maximum(jnp.maximum(ls[0], ls[1]), ls[2])
    es = [jnp.exp(l - lmax) for l in ls]
    yc_ref[...] = (es[0] * os_[0] + es[1] * os_[1] + es[2] * os_[2]) / (es[0] + es[1] + es[2])


def _sample_mixers(layer, r_s, p_s, state_gla, cache_b, caches_c, sink, wlr_pad, blr, ng):
    bs = r_s.shape[0]
    r3 = r_s.reshape(bs, U_PROJ, LANES)
    p3 = p_s.reshape(bs, U_PROJ, LANES)
    z4 = jnp.zeros((bs, 4, LANES), F32)

    def qrows(x3, ub, uc, rep):
        parts = [jnp.repeat(x3[:, ub:ub + B_KV_HEADS], B_GROUP, axis=1) if rep else x3[:, ub:ub + B_HEADS], z4]
        for g in range(N_C_GROUPS):
            parts += [x3[:, uc + C_HEADS * g:uc + C_HEADS * (g + 1)], z4]
        return jnp.concatenate(parts, axis=1)

    q_rows = qrows(r3, R_QB, R_QC, False)
    kn_rows = qrows(r3, R_KB, R_KC, True)
    vn_rows = qrows(p3, P_VB, P_VC, True)
    qk_a = p3[:, P_QA:P_VA]
    v_a = p_s[:, P_VA * LANES:P_GA * LANES].reshape(bs, A_HEADS, A_DV)
    g_a = p_s[:, P_GA * LANES:P_VB * LANES].reshape(bs, A_HEADS, A_DV)
    lr = jnp.broadcast_to(p3[:, P_LR:P_LR + 1], (bs, SUBLANES, LANES))

    cc = [c.reshape(bs, DEPTH, 2, ATTN_BLOCK, dil, C_HEADS, HEAD_DIM) for c, (_, dil) in zip(caches_c, C_PATTERNS)]

    def tok(rows, width):
        return pl.BlockSpec((None, rows, width), lambda b: (b, 0, 0))

    in_specs = [
        pl.BlockSpec(memory_space=pltpu.SMEM),
        tok(S_ROWS, LANES), tok(S_ROWS, LANES), tok(S_ROWS, LANES),
        tok(2 * A_HEADS, LANES), tok(A_HEADS, A_DV), tok(A_HEADS, A_DV), tok(SUBLANES, LANES),
        pl.BlockSpec((None, None, A_HEADS, A_DK, A_DV), lambda b: (b, layer, 0, 0, 0)),
        pl.BlockSpec((None, None, 2, B_WINDOW, B_KV_HEADS, HEAD_DIM), lambda b: (b, layer, 0, 0, 0, 0)),
    ] + [pl.BlockSpec((None, None, 2, ATTN_BLOCK, None, C_HEADS, HEAD_DIM), lambda b: (b, layer, 0, 0, 0, 0, 0)) for _ in cc] + [
        pl.BlockSpec((LANES, A_HEADS * A_DK), lambda b: (0, 0)),
        pl.BlockSpec((1, A_HEADS * A_DK), lambda b: (0, 0)),
        pl.BlockSpec((1, A_DV), lambda b: (0, 0)),
    ]
    out_specs = [
        pl.BlockSpec((None, A_HEADS, A_DV), lambda b: (b, 0, 0)),
        pl.BlockSpec((None, 16, HEAD_DIM), lambda b: (b, 0, 0)),
        pl.BlockSpec((None, SUBLANES, HEAD_DIM), lambda b: (b, 0, 0)),
        pl.BlockSpec((None, A_HEADS, A_DK, A_DV), lambda b: (b, 0, 0, 0)),
    ]
    out_shape = [
        jax.ShapeDtypeStruct((bs, A_HEADS, A_DV), F32),
        jax.ShapeDtypeStruct((bs, 16, HEAD_DIM), F32),
        jax.ShapeDtypeStruct((bs, SUBLANES, HEAD_DIM), F32),
        jax.ShapeDtypeStruct((bs, A_HEADS, A_DK, A_DV), F32),
    ]
    ya, ob, yc, sn = pl.pallas_call(
        _sample_body, grid=(bs,), in_specs=in_specs, out_specs=out_specs, out_shape=out_shape,
        compiler_params=_cparams(("parallel",), 32), name="sample_mixers",
    )(sink, q_rows, kn_rows, vn_rows, qk_a, v_a, g_a, lr, state_gla, cache_b, *cc, wlr_pad, blr, ng)
    return (ya.reshape(bs, A_HEADS * A_DV), ob[:, :B_HEADS].reshape(bs, B_HEADS * HEAD_DIM),
            yc[:, :C_HEADS].reshape(bs, C_HEADS * HEAD_DIM), sn)


def _outproj_body(ya_ref, ob_ref, yc_ref, wa_ref, wb_ref, wc_ref, bias_ref, h_ref, y_ref):
    acc = jnp.dot(ya_ref[...].astype(BF16), wa_ref[...], preferred_element_type=F32)
    acc = acc + jnp.dot(ob_ref[...].astype(BF16), wb_ref[...], preferred_element_type=F32)
    acc = acc + jnp.dot(yc_ref[...].astype(BF16), wc_ref[...], preferred_element_type=F32)
    y_ref[...] = DN_ALPHA * h_ref[...] + acc + bias_ref[...]


def _outproj(ya, ob, yc, wa, wb, wc, bias, h, *, tm, tn=1024):
    m = h.shape[0]

    def rowspec(x):
        return pl.BlockSpec((tm, x.shape[1]), lambda j, i: (i, 0))

    def wspec(w):
        return pl.BlockSpec((w.shape[0], tn), lambda j, i: (0, j))

    return pl.pallas_call(
        _outproj_body, grid=(D_MODEL // tn, m // tm),
        in_specs=[rowspec(ya), rowspec(ob), rowspec(yc), wspec(wa), wspec(wb), wspec(wc),
                  pl.BlockSpec((1, tn), lambda j, i: (0, j)), pl.BlockSpec((tm, tn), lambda j, i: (i, j))],
        out_specs=pl.BlockSpec((tm, tn), lambda j, i: (i, j)),
        out_shape=jax.ShapeDtypeStruct((m, D_MODEL), F32),
        compiler_params=_cparams(("parallel", "parallel"), 48), name="outproj",
    )(ya, ob, yc, wa, wb, wc, bias, h)


def _pack_bf16_pairs(x):
    half = x.shape[1] // 2
    bits = lax.bitcast_convert_type(x.astype(BF16).astype(F32), jnp.uint32)
    return (bits[:, :half] >> 16) | bits[:, half:]


def _unpack_bf16_pairs(w):
    lo = lax.bitcast_convert_type(w << 16, F32).astype(BF16)
    hi = lax.bitcast_convert_type(w & jnp.uint32(0xFFFF0000), F32).astype(BF16)
    return lo, hi


def _ln_router_body(y_ref, g_ref, b_ref, rw_ref, rb_ref, h_ref, pk_ref, lg_ref, *, n_real):
    i = pl.program_id(0)

    @pl.when(i < n_real)
    def _():
        h = _ln_rows(y_ref[...], g_ref[...], b_ref[...])
        h_ref[...] = h
        pk_ref[...] = _pack_bf16_pairs(h)
        lg_ref[...] = jnp.dot(h, rw_ref[...], precision=HIGHEST, preferred_element_type=F32) + rb_ref[...]

    @pl.when(i >= n_real)
    def _():
        h_ref[...] = jnp.zeros_like(h_ref)
        pk_ref[...] = jnp.zeros_like(pk_ref)
        lg_ref[...] = jnp.zeros_like(lg_ref)


def _ln_router(y, g, b, rw_pad, rb_pad, tm, extra_blocks=0):
    m, d = y.shape
    n_real = m // tm
    n_out = n_real + extra_blocks
    row_in = pl.BlockSpec((tm, d), lambda i: (jnp.minimum(i, n_real - 1), 0))
    vec = pl.BlockSpec((1, d), lambda i: (0, 0))
    return pl.pallas_call(
        functools.partial(_ln_router_body, n_real=n_real), grid=(n_out,),
        in_specs=[row_in, vec, vec, pl.BlockSpec((d, LANES), lambda i: (0, 0)), pl.BlockSpec((1, LANES), lambda i: (0, 0))],
        out_specs=[pl.BlockSpec((tm, d), lambda i: (i, 0)), pl.BlockSpec((tm, d // 2), lambda i: (i, 0)),
                   pl.BlockSpec((tm, LANES), lambda i: (i, 0))],
        out_shape=[jax.ShapeDtypeStruct((n_out * tm, d), F32), jax.ShapeDtypeStruct((n_out * tm, d // 2), jnp.uint32),
                   jax.ShapeDtypeStruct((n_out * tm, LANES), F32)],
        compiler_params=_cparams(("arbitrary",), 48), name="ln_router",
    )(y, g, b, rw_pad, rb_pad)


def _moe_gather_body(src_ref, na_ref, h_hbm, xs_ref, buf, sem, *, tm):
    i = pl.program_id(0)
    na = na_ref[0]

    def row_copy(blk, slot, r):
        return pltpu.make_async_copy(h_hbm.at[pl.ds(src_ref[blk * tm + r], 1), :], buf.at[slot, pl.ds(r, 1), :],
                                     sem.at[slot])

    def issue(blk, slot):
        def body(r2, carry):
            row_copy(blk, slot, 2 * r2).start(priority=0)
            row_copy(blk, slot, 2 * r2 + 1).start(priority=1)
            return carry
        lax.fori_loop(0, tm // 2, body, 0)

    @pl.when(i == 0)
    def _():
        issue(0, 0)

    slot = i % 2

    @pl.when(i + 1 < na)
    def _():
        issue(i + 1, 1 - slot)

    @pl.when(i < na)
    def _():
        def wait_body(r, carry):
            row_copy(i, slot, r).wait()
            return carry
        lax.fori_loop(0, tm, wait_body, 0)
        half = buf.shape[2]
        lo, hi = _unpack_bf16_pairs(buf[slot])
        xs_ref[:, :half] = lo
        xs_ref[:, half:] = hi

    @pl.when(i >= na)
    def _():
        xs_ref[...] = jnp.zeros_like(xs_ref)


def _moe_gather(h_packed, src, n_active, tm):
    half = h_packed.shape[1]
    n_blk = src.shape[0] // tm
    assert tm % 2 == 0
    return pl.pallas_call(
        functools.partial(_moe_gather_body, tm=tm),
        grid_spec=pltpu.PrefetchScalarGridSpec(
            num_scalar_prefetch=2, grid=(n_blk,), in_specs=[pl.BlockSpec(memory_space=pl.ANY)],
            out_specs=pl.BlockSpec((tm, 2 * half), lambda i, sr, na: (i, 0)),
            scratch_shapes=[pltpu.VMEM((2, tm, half), jnp.uint32), pltpu.SemaphoreType.DMA((2,))]),
        out_shape=jax.ShapeDtypeStruct((n_blk * tm, 2 * half), BF16),
        compiler_params=_cparams(("arbitrary",), 40), name="moe_gather",
    )(src, n_active, h_packed)


def _weight_stream(w_hbms, stages, sem, be_ref, nb_ref, na_ref, cnt_ref, *, layer, tn, on_ready):
    j = pl.program_id(0)
    i = pl.program_id(1)
    n_j = pl.num_programs(0)
    na = na_ref[0]
    active = i < na
    fresh = active & ((i == 0) | (be_ref[i] != be_ref[jnp.maximum(i - 1, 0)]))

    def copies(e, jj, slot):
        cols = pl.ds(pl.multiple_of(jj * tn, tn), tn)
        return [pltpu.make_async_copy(w.at[layer, e, :, cols], st.at[slot], sem.at[t, slot])
                for t, (w, st) in enumerate(zip(w_hbms, stages))]

    @pl.when((j == 0) & (i == 0))
    def _():
        cnt_ref[0] = 0
        for cp in copies(be_ref[0], 0, 0):
            cp.start()

    @pl.when(fresh)
    def _():
        slot = cnt_ref[0] % 2
        for cp in copies(be_ref[i], j, slot):
            cp.wait()
        nxt = nb_ref[i]
        same_j = nxt < na
        e_next = be_ref[jnp.where(same_j, nxt, 0)]
        j_next = jnp.where(same_j, j, j + 1)

        @pl.when(same_j | (j + 1 < n_j))
        def _():
            for cp in copies(e_next, j_next, 1 - slot):
                cp.start()

        on_ready(slot)
        cnt_ref[0] = cnt_ref[0] + 1

    return active


def _moe_up_body(be_ref, nb_ref, na_ref, x_ref, bg_ref, bu_ref, wg_hbm, wu_hbm, a_ref,
                 stage_g, stage_u, wg_bf, wu_bf, sem, cnt_ref, *, layer, tn):
    def on_ready(slot):
        wg_bf[...] = stage_g[slot].astype(BF16)
        wu_bf[...] = stage_u[slot].astype(BF16)

    active = _weight_stream((wg_hbm, wu_hbm), (stage_g, stage_u), sem, be_ref, nb_ref, na_ref, cnt_ref,
                            layer=layer, tn=tn, on_ready=on_ready)

    @pl.when(active)
    def _():
        x = x_ref[...]
        g = jnp.dot(x, wg_bf[...], preferred_element_type=F32) + bg_ref[...]
        u = jnp.dot(x, wu_bf[...], preferred_element_type=F32) + bu_ref[...]
        g = jnp.minimum(g, SWIGLU_LIMIT)
        u = jnp.clip(u, -SWIGLU_LIMIT, SWIGLU_LIMIT)
        a_ref[...] = (g * jax.nn.sigmoid(SWIGLU_ALPHA * g) * (u + 1.0)).astype(BF16)

    @pl.when(jnp.logical_not(active))
    def _():
        a_ref[...] = jnp.zeros_like(a_ref)


def _moe_down_body(be_ref, nb_ref, na_ref, a_ref, bd_ref, wd_hbm, y_ref, stage_d, wd_bf, sem, cnt_ref, *, layer, tn):
    def on_ready(slot):
        wd_bf[...] = stage_d[slot].astype(BF16)

    active = _weight_stream((wd_hbm,), (stage_d,), sem, be_ref, nb_ref, na_ref, cnt_ref,
                            layer=layer, tn=tn, on_ready=on_ready)

    @pl.when(active)
    def _():
        y_ref[...] = jnp.dot(a_ref[...], wd_bf[...], preferred_element_type=F32) + bd_ref[...]

    @pl.when(jnp.logical_not(active))
    def _():
        y_ref[...] = jnp.zeros_like(y_ref)


def _moe_block_rows(n_rows):
    mean = n_rows / N_EXPERTS
    return int(-(-(mean + 2.0 * mean ** 0.5) // (3 * 2 * SUBLANES)) * 2 * SUBLANES)


def _moe_experts(layer, xs, block_expert, next_block, n_active, w_gate, b_gate, w_up, b_up, w_down, b_down, tm):
    m_pad, d = xs.shape
    d_ff = w_gate.shape[-1]
    n_blk = m_pad // tm

    def blk(i, na):
        return jnp.minimum(i, na[0] - 1)

    def rows(width):
        return pl.BlockSpec((tm, width), lambda j, i, be, nb, na: (blk(i, na), 0))

    def bias(t):
        return pl.BlockSpec((None, None, 1, t), lambda j, i, be, nb, na: (layer, be[i], 0, j))

    def out(t):
        return pl.BlockSpec((tm, t), lambda j, i, be, nb, na: (i, j))

    hbm = pl.BlockSpec(memory_space=pl.ANY)
    smem_cnt = pltpu.SMEM((1,), jnp.int32)

    tn = MOE_TN_UP
    act = pl.pallas_call(
        functools.partial(_moe_up_body, layer=layer, tn=tn),
        grid_spec=pltpu.PrefetchScalarGridSpec(
            num_scalar_prefetch=3, grid=(d_ff // tn, n_blk),
            in_specs=[rows(d), bias(tn), bias(tn), hbm, hbm],
            out_specs=out(tn),
            scratch_shapes=[pltpu.VMEM((2, d, tn), F32), pltpu.VMEM((2, d, tn), F32),
                            pltpu.VMEM((d, tn), BF16), pltpu.VMEM((d, tn), BF16),
                            pltpu.SemaphoreType.DMA((2, 2)), smem_cnt]),
        out_shape=jax.ShapeDtypeStruct((m_pad, d_ff), BF16),
        compiler_params=_cparams(("arbitrary", "arbitrary"), 56), name="moe_gate_up",
    )(block_expert, next_block, n_active, xs,
      b_gate.reshape(DEPTH, N_EXPERTS, 1, d_ff), b_up.reshape(DEPTH, N_EXPERTS, 1, d_ff), w_gate, w_up)

    tn2 = MOE_TN_DOWN
    return pl.pallas_call(
        functools.partial(_moe_down_body, layer=layer, tn=tn2),
        grid_spec=pltpu.PrefetchScalarGridSpec(
            num_scalar_prefetch=3, grid=(d // tn2, n_blk),
            in_specs=[rows(d_ff), bias(tn2), hbm],
            out_specs=out(tn2),
            scratch_shapes=[pltpu.VMEM((2, d_ff, tn2), F32), pltpu.VMEM((d_ff, tn2), BF16),
                            pltpu.SemaphoreType.DMA((1, 2)), smem_cnt]),
        out_shape=jax.ShapeDtypeStruct((m_pad, d), F32),
        compiler_params=_cparams(("arbitrary", "arbitrary"), 56), name="moe_down",
    )(block_expert, next_block, n_active, act, b_down.reshape(DEPTH, N_EXPERTS, 1, d), w_down)


def _combine_body(dest_ref, h_ref, gt_ref, g_ref, b_ref, yb_hbm, o_ref, ob_ref, yg, sem, *, tm):
    t = pl.program_id(0)
    n_tiles = pl.num_programs(0)

    def row_copy(tile, slot, r, kk):
        d = dest_ref[(tile * tm + r) * TOP_K + kk]
        return pltpu.make_async_copy(yb_hbm.at[pl.ds(d, 1), :], yg.at[slot, kk, pl.ds(r, 1), :], sem.at[slot])

    def issue(tile, slot):
        def body(r, carry):
            for kk in range(TOP_K):
                row_copy(tile, slot, r, kk).start(priority=kk % 2)
            return carry
        lax.fori_loop(0, tm, body, 0)

    @pl.when(t == 0)
    def _():
        issue(0, 0)

    slot = t % 2

    @pl.when(t + 1 < n_tiles)
    def _():
        issue(t + 1, 1 - slot)

    def wait_body(r, carry):
        for kk in range(TOP_K):
            row_copy(t, slot, r, kk).wait()
        return carry
    lax.fori_loop(0, tm, wait_body, 0)

    gt = gt_ref[...]
    moe = ((yg[slot, 0] * gt[:, 0:1] + yg[slot, 1] * gt[:, 1:2])
           + (yg[slot, 2] * gt[:, 2:3] + yg[slot, 3] * gt[:, 3:4]))
    y = _ln_rows(DN_ALPHA * h_ref[...] + moe, g_ref[...], b_ref[...])
    o_ref[...] = y
    ob_ref[...] = y.astype(BF16)


def _combine_ln(h1, yb, dest, gates, g, b, tm, m):
    d = h1.shape[1]
    row = pl.BlockSpec((tm, d), lambda i, dr: (i, 0))
    vec = pl.BlockSpec((1, d), lambda i, dr: (0, 0))
    return pl.pallas_call(
        functools.partial(_combine_body, tm=tm),
        grid_spec=pltpu.PrefetchScalarGridSpec(
            num_scalar_prefetch=1, grid=(m // tm,),
            in_specs=[row, pl.BlockSpec((tm, TOP_K), lambda i, dr: (i, 0)), vec, vec,
                      pl.BlockSpec(memory_space=pl.ANY)],
            out_specs=[row, row],
            scratch_shapes=[pltpu.VMEM((2, TOP_K, tm, d), F32), pltpu.SemaphoreType.DMA((2,))]),
        out_shape=[jax.ShapeDtypeStruct((m, d), F32), jax.ShapeDtypeStruct((m, d), BF16)],
        compiler_params=_cparams(("arbitrary",), 48), name="combine_ln2",
    )(dest, h1, gates, g, b, yb)


def _route(logits, tm):
    n_tok = logits.shape[0]
    top_v, top_e = lax.top_k(logits, TOP_K)
    gates = jax.nn.softmax(top_v, axis=-1)
    m = n_tok * TOP_K
    e_flat = top_e.reshape(m)
    n_blk = -(-m // tm) + N_EXPERTS
    onehot = jax.nn.one_hot(e_flat, N_EXPERTS, dtype=jnp.int32)
    counts = jnp.sum(onehot, axis=0)
    first = jnp.cumsum(counts) - counts
    order = jnp.argsort(e_flat).astype(jnp.int32)
    pos = jnp.argsort(order).astype(jnp.int32)
    blocks_per = (counts + tm - 1) // tm
    blocks_end = jnp.cumsum(blocks_per)
    row0 = (blocks_end - blocks_per) * tm
    dest = pos + jnp.sum(onehot * (row0 - first)[None, :], axis=1)
    n_active = blocks_end[-1]
    bidx = jnp.minimum(jnp.arange(n_blk), n_active - 1)
    block_expert = jnp.minimum(jnp.sum(bidx[:, None] >= blocks_end[None, :], axis=1), N_EXPERTS - 1).astype(jnp.int32)
    next_block = blocks_end[block_expert].astype(jnp.int32)
    off = (jnp.arange(n_blk)[:, None] * tm - row0[block_expert][:, None]) + jnp.arange(tm)[None, :]
    valid = (off < counts[block_expert][:, None]) & (jnp.arange(n_blk)[:, None] < n_active)
    sorted_pos = jnp.clip(first[block_expert][:, None] + off, 0, m - 1)
    src = jnp.where(valid, order[sorted_pos] // TOP_K, 0).reshape(n_blk * tm)
    return gates, dest.astype(jnp.int32), src, block_expert, next_block, n_active.reshape(1).astype(jnp.int32)


def _rope_tables(pos):
    half = HEAD_DIM // 2
    inv_freq = ROPE_THETA ** (-np.arange(half, dtype=np.float64) / half)
    ang = np.asarray(pos, np.float64)[:, None] * inv_freq[None, :]
    cos, sin = np.cos(ang), np.sin(ang)
    return (jnp.asarray(np.concatenate([cos, cos], axis=1), F32),
            jnp.asarray(np.concatenate([-sin, sin], axis=1), F32))


def _split_w_in(w, b):
    k = w.shape[0]
    w = w.astype(BF16)
    w_r = jnp.concatenate([w[:, _O_QB:_O_VB], w[:, _O_QC:_O_VC], jnp.zeros((k, LANES), w.dtype)], axis=1)
    w_p = jnp.concatenate([w[:, _O_QA:_O_LR], w[:, _O_VB:_O_QC], w[:, _O_VC:_O_END], w[:, _O_LR:_O_QB],
                           jnp.zeros((k, LANES - A_LOWRANK), w.dtype)], axis=1)
    b_r = jnp.concatenate([b[_O_QB:_O_VB], b[_O_QC:_O_VC], jnp.zeros((LANES,), b.dtype)])
    b_p = jnp.concatenate([b[_O_QA:_O_LR], b[_O_VB:_O_QC], b[_O_VC:_O_END], b[_O_LR:_O_QB],
                           jnp.zeros((LANES - A_LOWRANK,), b.dtype)])
    return w_r, w_p, b_r.reshape(1, W_PROJ), b_p.reshape(1, W_PROJ)


def _mix_prompt(rp, pp, wlr_pad, blr, ng, sink, bp, tlen):
    ya, st = _gla_prompt(pp, wlr_pad, blr, ng, bp, tlen)
    ob = _swa_attention(rp, pp, sink, bp, tlen)
    yc = _dilated_attention(rp, pp, bp, tlen)
    return ya, ob, yc, st


def kernel(x_prompt, x_sample, cache_b_kv, cache_c1_kv, cache_c2_kv, cache_c3_kv, state_gla, ln_in_g, ln_in_b, w_in, b_in, gla_w_gate, gla_b_gate, gla_norm_g, attn_sinks, w_out, b_out, ln1_g, ln1_b, router_w, router_b, w_gate, b_gate, w_up, b_up, w_down, b_down, ln2_g, ln2_b):
    bp, tlen, d = x_prompt.shape
    bs, slen, _ = x_sample.shape
    assert slen == 1 and d == D_MODEL and tlen % SWA_ROWS == 0
    n_p, n_s = bp * tlen, bs * slen
    tm_p = 256

    cos_p, sin_p = _rope_tables(np.tile(np.arange(tlen), bp))
    cos_s, sin_s = _rope_tables(np.full((n_s,), PAST_LEN))
    caches_c = (cache_c1_kv, cache_c2_kv, cache_c3_kv)

    hp, hp_b = _ln_in(x_prompt.reshape(n_p, d), ln_in_g, ln_in_b, tm_p)
    hs, hs_b = _ln_in(x_sample.reshape(n_s, d), ln_in_g, ln_in_b, n_s)

    new_p = [[] for _ in range(5)]
    new_s = [[] for _ in range(5)]
    for l in range(DEPTH):
        w_r, w_p, b_r, b_p = _split_w_in(w_in[l], b_in[l])
        wlr_pad = jnp.concatenate([gla_w_gate[l], jnp.zeros((LANES - A_LOWRANK, A_HEADS * A_DK), F32)], axis=0)
        blr = gla_b_gate[l].reshape(1, A_HEADS * A_DK)
        ng = gla_norm_g[l].reshape(1, A_DV)
        sink = attn_sinks[l].reshape(B_HEADS)
        wo = w_out[l].astype(BF16)
        na, nb = A_HEADS * A_DV, B_HEADS * HEAD_DIM
        wo_a, wo_b, wo_c = wo[:na], wo[na:na + nb], wo[na + nb:]
        bo = b_out[l].reshape(1, d)

        rp = _proj(hp_b, w_r, b_r, cos_p, sin_p, tm=512)
        pp = _proj(hp_b, w_p, b_p, tm=512)
        rs = _proj(hs_b, w_r, b_r, cos_s, sin_s, tm=n_s, tn=W_PROJ // 2)
        ps = _proj(hs_b, w_p, b_p, tm=n_s, tn=W_PROJ // 2)

        ya_p, ob_p, yc_p, st_p = _mix_prompt(rp, pp, wlr_pad, blr, ng, sink, bp, tlen)
        ya_s, ob_s, yc_s, st_s = _sample_mixers(l, rs, ps, state_gla, cache_b_kv, caches_c, sink, wlr_pad, blr, ng)

        rp4 = rp.reshape(bp, tlen, W_PROJ)
        pp4 = pp.reshape(bp, tlen, W_PROJ)

        def kv_prompt(uk, uv, heads, rows):
            kk = rp4[:, tlen - rows:, uk * LANES:(uk + heads) * LANES].reshape(bp, rows, heads, HEAD_DIM)
            vv = pp4[:, tlen - rows:, uv * LANES:(uv + heads) * LANES].reshape(bp, rows, heads, HEAD_DIM)
            return jnp.stack([kk, vv], axis=1)

        def kv_sample(uk, uv, heads):
            kk = rs[:, uk * LANES:(uk + heads) * LANES].reshape(bs, slen, heads, HEAD_DIM)
            vv = ps[:, uv * LANES:(uv + heads) * LANES].reshape(bs, slen, heads, HEAD_DIM)
            return jnp.stack([kk, vv], axis=1)

        new_p[0].append(kv_prompt(R_KB, P_VB, B_KV_HEADS, min(B_WINDOW, tlen)))
        new_s[0].append(kv_sample(R_KB, P_VB, B_KV_HEADS))
        for g, (win, dil) in enumerate(C_PATTERNS):
            new_p[1 + g].append(kv_prompt(R_KC + C_HEADS * g, P_VC + C_HEADS * g, C_HEADS, min(win, tlen)))
            new_s[1 + g].append(kv_sample(R_KC + C_HEADS * g, P_VC + C_HEADS * g, C_HEADS))
        new_p[4].append(st_p)
        new_s[4].append(st_s)

        y_p = _outproj(ya_p, ob_p, yc_p, wo_a, wo_b, wo_c, bo, hp, tm=512)
        y_s = _outproj(ya_s, ob_s, yc_s, wo_a, wo_b, wo_c, bo, hs, tm=n_s, tn=D_MODEL // 2)
        rw_pad = jnp.concatenate([router_w[l], jnp.zeros((d, LANES - N_EXPERTS), F32)], axis=1)
        rb_pad = jnp.concatenate([router_b[l], jnp.zeros((LANES - N_EXPERTS,), F32)]).reshape(1, LANES)
        g1, b1 = ln1_g[l].reshape(1, d), ln1_b[l].reshape(1, d)
        h1_p, pk_p, lg_p = _ln_router(y_p, g1, b1, rw_pad, rb_pad, tm_p, extra_blocks=1)
        h1_s, pk_s, lg_s = _ln_router(y_s, g1, b1, rw_pad, rb_pad, n_s)
        pk_all = lax.dynamic_update_slice(pk_p, pk_s, (n_p, 0))

        logits = jnp.concatenate([lg_p[:n_p], lg_s], axis=0)[:, :N_EXPERTS]
        moe_tm = _moe_block_rows((n_p + n_s) * TOP_K)
        gates, dest, src, block_expert, next_block, n_active = _route(logits, moe_tm)
        xs = _moe_gather(pk_all, src, n_active, moe_tm)
        yb = _moe_experts(l, xs, block_expert, next_block, n_active, w_gate, b_gate, w_up, b_up, w_down, b_down, moe_tm)
        g2, b2 = ln2_g[l].reshape(1, d), ln2_b[l].reshape(1, d)
        hp, hp_b = _combine_ln(h1_p, yb, dest[:n_p * TOP_K], gates[:n_p], g2, b2, 128, n_p)
        hs, hs_b = _combine_ln(h1_s, yb, dest[n_p * TOP_K:], gates[n_p:], g2, b2, n_s, n_s)

    outs = [hp.reshape(bp, tlen, d), hs.reshape(bs, slen, d)]
    for i in range(5):
        outs += [jnp.stack(new_p[i], axis=1), jnp.stack(new_s[i], axis=1)]
    return tuple(outs)
```

```python
import functools

import numpy as np
import jax
import jax.numpy as jnp
from jax import lax
from jax.experimental import pallas as pl
from jax.experimental.pallas import tpu as pltpu

F32 = jnp.float32
BF16 = jnp.bfloat16
HIGHEST = lax.Precision.HIGHEST

LANES = 128
SUBLANES = 8

D_MODEL = 4096
DEPTH = 2
PAST_LEN = 16384
HEAD_DIM = 128
A_HEADS, A_DK, A_DV, A_LOWRANK = 4, 128, 256, 16
A_GATE_NORM = 16.0
A_CHUNK = 64
B_HEADS, B_KV_HEADS = 12, 3
B_GROUP = B_HEADS // B_KV_HEADS
B_WINDOW = 128
C_PATTERNS = ((128, 1), (512, 4), (2048, 16))
N_C_GROUPS = len(C_PATTERNS)
C_HEADS = 4
ATTN_BLOCK = 128
ROPE_THETA = 10000.0
N_EXPERTS = 32
TOP_K = 4
SWIGLU_LIMIT = 7.0
SWIGLU_ALPHA = 1.702
DN_ALPHA = (2 * DEPTH) ** 0.25
LN_EPS = 1e-5
RMS_EPS = 1e-6
NEG_INF = -1e30
ATTN_SCALE = HEAD_DIM ** -0.5

_O_QA, _O_KA, _O_VA, _O_GA, _O_LR, _O_QB, _O_KB, _O_VB, _O_QC, _O_KC, _O_VC, _O_END = (
    0, 512, 1024, 2048, 3072, 3088, 4624, 5008, 5392, 6928, 8464, 10000)
R_QB, R_KB, R_QC, R_KC = 0, 12, 15, 27
P_QA, P_KA, P_VA, P_GA, P_VB, P_VC, P_LR = 0, 4, 8, 16, 24, 27, 39
U_PROJ = 40
W_PROJ = U_PROJ * LANES

S_ROWS = 16 + SUBLANES * N_C_GROUPS

MOE_TN_UP = 512
MOE_TN_DOWN = 1024
SWA_ROWS = 512


def _cparams(sem, vmem_mb):
    return pltpu.CompilerParams(dimension_semantics=sem, vmem_limit_bytes=vmem_mb * 1024 * 1024)


def _ln_rows(x, g, b):
    mu = jnp.mean(x, axis=-1, keepdims=True)
    xc = x - mu
    var = jnp.mean(xc * xc, axis=-1, keepdims=True)
    return xc * lax.rsqrt(var + LN_EPS) * g + b


def _ln_in_body(x_ref, g_ref, b_ref, o_ref, ob_ref):
    y = _ln_rows(x_ref[...], g_ref[...], b_ref[...])
    o_ref[...] = y
    ob_ref[...] = y.astype(BF16)


def _ln_in(x, g, b, tm):
    m, d = x.shape
    row = pl.BlockSpec((tm, d), lambda i: (i, 0))
    vec = pl.BlockSpec((1, d), lambda i: (0, 0))
    return pl.pallas_call(
        _ln_in_body, grid=(m // tm,), in_specs=[row, vec, vec], out_specs=[row, row],
        out_shape=[jax.ShapeDtypeStruct((m, d), F32), jax.ShapeDtypeStruct((m, d), BF16)],
        compiler_params=_cparams(("parallel",), 48), name="ln_in",
    )(x, g.reshape(1, d), b.reshape(1, d))


def _proj_body(*refs, rope, tn):
    if rope:
        x_ref, w_ref, b_ref, cos_ref, sin_ref, o_ref = refs
    else:
        x_ref, w_ref, b_ref, o_ref = refs
    acc = jnp.dot(x_ref[...], w_ref[...], preferred_element_type=F32) + b_ref[...]
    if not rope:
        o_ref[...] = acc
        return
    cos = cos_ref[...]
    sin = sin_ref[...]
    for u in range(tn // LANES):
        seg = acc[:, u * LANES:(u + 1) * LANES]
        o_ref[:, u * LANES:(u + 1) * LANES] = seg * cos + pltpu.roll(seg, HEAD_DIM // 2, 1) * sin


def _proj(xb, w, b, cos=None, sin=None, *, tm, tn=1024):
    m, k = xb.shape
    n = w.shape[1]
    rope = cos is not None
    in_specs = [pl.BlockSpec((tm, k), lambda j, i: (i, 0)),
                pl.BlockSpec((k, tn), lambda j, i: (0, j)),
                pl.BlockSpec((1, tn), lambda j, i: (0, j))]
    args = [xb, w, b]
    if rope:
        tab = pl.BlockSpec((tm, LANES), lambda j, i: (i, 0))
        in_specs += [tab, tab]
        args += [cos, sin]
    return pl.pallas_call(
        functools.partial(_proj_body, rope=rope, tn=tn), grid=(n // tn, m // tm),
        in_specs=in_specs, out_specs=pl.BlockSpec((tm, tn), lambda j, i: (i, j)),
        out_shape=jax.ShapeDtypeStruct((m, n), F32),
        compiler_params=_cparams(("parallel", "parallel"), 48),
        name="proj_rope" if rope else "proj_plain",
    )(*args)


def _log_sigmoid(z):
    return jnp.minimum(z, 0.0) - jnp.log(1.0 + jnp.exp(-jnp.abs(z)))


def _rms_gate(o, ng, ga):
    o = o * lax.rsqrt(jnp.mean(o * o, axis=-1, keepdims=True) + RMS_EPS) * ng
    return o * (ga * jax.nn.sigmoid(ga))


def _gla_body(q_ref, k_ref, v_ref, ga_ref, lr_ref, wlr_ref, blr_ref, ng_ref, ya_ref, st_ref, s_scr, *, n_sub):
    c = pl.program_id(2)

    @pl.when(c == 0)
    def _():
        s_scr[...] = jnp.zeros_like(s_scr)

    ch = A_CHUNK
    rr = lax.broadcasted_iota(jnp.int32, (ch, ch), 0)
    cc = lax.broadcasted_iota(jnp.int32, (ch, ch), 1)
    causal = rr >= cc
    tri = causal.astype(F32)
    ones = jnp.ones((ch, A_DK), F32)
    for u in range(n_sub):
        sl = pl.ds(u * ch, ch)
        z = jnp.dot(lr_ref[sl, :], wlr_ref[...], precision=HIGHEST, preferred_element_type=F32) + blr_ref[...]
        g = _log_sigmoid(z) / A_GATE_NORM
        b = jnp.dot(tri, g, precision=HIGHEST, preferred_element_type=F32)
        b_last = b[ch - 1:ch, :]
        b_last_col = lax.dot_general(g, ones, (((0,), (0,)), ((), ())), precision=HIGHEST,
                                     preferred_element_type=F32)
        q = q_ref[sl, :] * (A_DK ** -0.5)
        k = k_ref[sl, :]
        v = v_ref[sl, :].astype(BF16)
        qg = (q * jnp.exp(b)).astype(BF16)
        kg = (k * jnp.exp(-b)).astype(BF16)
        kd = (k * jnp.exp(b_last - b)).astype(BF16)
        att = lax.dot_general(qg, kg, (((1,), (1,)), ((), ())), preferred_element_type=F32)
        att = jnp.where(causal, att, 0.0).astype(BF16)
        s = s_scr[...]
        o = (jnp.dot(qg, s.astype(BF16), preferred_element_type=F32)
             + jnp.dot(att, v, preferred_element_type=F32))
        decay = jnp.exp(b_last_col)
        decay = jnp.concatenate([decay, decay], axis=1)
        s_scr[...] = decay * s + lax.dot_general(kd, v, (((0,), (0,)), ((), ())), preferred_element_type=F32)
        ya_ref[sl, :] = _rms_gate(o, ng_ref[...], ga_ref[sl, :])

    @pl.when(c == pl.num_programs(2) - 1)
    def _():
        st_ref[...] = s_scr[...]


def _gla_prompt(p_arr, wlr_pad, blr, ng, bsz, tlen, rb=256):
    nblk = tlen // rb
    n_sub = rb // A_CHUNK

    def rows(b, c):
        return b * nblk + c

    in_specs = [
        pl.BlockSpec((rb, A_DK), lambda b, h, c: (rows(b, c), P_QA + h)),
        pl.BlockSpec((rb, A_DK), lambda b, h, c: (rows(b, c), P_KA + h)),
        pl.BlockSpec((rb, A_DV), lambda b, h, c: (rows(b, c), P_VA // 2 + h)),
        pl.BlockSpec((rb, A_DV), lambda b, h, c: (rows(b, c), P_GA // 2 + h)),
        pl.BlockSpec((rb, LANES), lambda b, h, c: (rows(b, c), P_LR)),
        pl.BlockSpec((LANES, A_DK), lambda b, h, c: (0, h)),
        pl.BlockSpec((1, A_DK), lambda b, h, c: (0, h)),
        pl.BlockSpec((1, A_DV), lambda b, h, c: (0, 0)),
    ]
    out_specs = [
        pl.BlockSpec((rb, A_DV), lambda b, h, c: (rows(b, c), h)),
        pl.BlockSpec((None, None, A_DK, A_DV), lambda b, h, c: (b, h, 0, 0)),
    ]
    return pl.pallas_call(
        functools.partial(_gla_body, n_sub=n_sub), grid=(bsz, A_HEADS, nblk),
        in_specs=in_specs, out_specs=out_specs,
        out_shape=[jax.ShapeDtypeStruct((bsz * tlen, A_HEADS * A_DV), F32),
                   jax.ShapeDtypeStruct((bsz, A_HEADS, A_DK, A_DV), F32)],
        scratch_shapes=[pltpu.VMEM((A_DK, A_DV), F32)],
        compiler_params=_cparams(("parallel", "parallel", "arbitrary"), 32), name="gla_prompt",
    )(p_arr, p_arr, p_arr, p_arr, p_arr, wlr_pad, blr, ng)


def _attn_block(q16, k_prev, k_cur, v_prev, v_cur, prev_on, sink):
    blk = ATTN_BLOCK
    rows = q16.shape[0]
    nk = blk if k_prev is None else 2 * blk
    r = lax.broadcasted_iota(jnp.int32, (rows, nk), 0)
    if rows > blk:
        r = r % blk
    c = lax.broadcasted_iota(jnp.int32, (rows, nk), 1)
    if k_prev is None:
        keys, vals = k_cur.astype(BF16), v_cur.astype(BF16)
        mask = c <= r
    else:
        keys = jnp.concatenate([k_prev, k_cur], axis=0).astype(BF16)
        vals = jnp.concatenate([v_prev, v_cur], axis=0).astype(BF16)
        in_prev = (c < blk) & (c >= r)
        if prev_on is not True:
            in_prev = in_prev & prev_on
        mask = in_prev | ((c >= blk) & (c - blk <= r))
    s = lax.dot_general(q16, keys, (((1,), (1,)), ((), ())), preferred_element_type=F32) * ATTN_SCALE
    s = jnp.where(mask, s, NEG_INF)
    m = jnp.max(s, axis=1, keepdims=True)
    if sink is not None:
        m = jnp.maximum(m, sink)
    p = jnp.exp(s - m)
    den = jnp.sum(p, axis=1, keepdims=True)
    if sink is not None:
        den = den + jnp.exp(sink - m)
    o = jnp.dot(p.astype(BF16), vals, preferred_element_type=F32) / den
    return o, m + jnp.log(den)


def _swa_body(sink_ref, q_ref, kc_ref, vc_ref, kp_ref, vp_ref, o_ref):
    blk = ATTN_BLOCK
    hk = pl.program_id(1)
    n = pl.program_id(2)
    sink = jnp.concatenate([jnp.full((blk, 1), sink_ref[hk * B_GROUP + g], F32) for g in range(B_GROUP)], axis=0)
    for i in range(SWA_ROWS // blk):
        rows = slice(i * blk, (i + 1) * blk)
        q = q_ref[rows, :]
        q16 = jnp.concatenate([q[:, g * HEAD_DIM:(g + 1) * HEAD_DIM] for g in range(B_GROUP)], axis=0).astype(BF16)
        if i == 0:
            k_prev, v_prev, prev_on = kp_ref[...], vp_ref[...], n > 0
        else:
            prev = slice((i - 1) * blk, i * blk)
            k_prev, v_prev, prev_on = kc_ref[prev, :], vc_ref[prev, :], True
        o, _ = _attn_block(q16, k_prev, kc_ref[rows, :], v_prev, vc_ref[rows, :], prev_on, sink)
        o_ref[rows, :] = jnp.concatenate([o[g * blk:(g + 1) * blk, :] for g in range(B_GROUP)], axis=1)


def _swa_attention(r_arr, p_arr, sink, bsz, tlen):
    nsb = tlen // SWA_ROWS
    sub = SWA_ROWS // ATTN_BLOCK
    gw = B_GROUP * HEAD_DIM

    def prev_blk(b, n):
        return (b * nsb + n) * sub - jnp.minimum(n, 1)

    in_specs = [
        pl.BlockSpec(memory_space=pltpu.SMEM),
        pl.BlockSpec((SWA_ROWS, gw), lambda b, h, n: (b * nsb + n, R_QB // B_GROUP + h)),
        pl.BlockSpec((SWA_ROWS, HEAD_DIM), lambda b, h, n: (b * nsb + n, R_KB + h)),
        pl.BlockSpec((SWA_ROWS, HEAD_DIM), lambda b, h, n: (b * nsb + n, P_VB + h)),
        pl.BlockSpec((ATTN_BLOCK, HEAD_DIM), lambda b, h, n: (prev_blk(b, n), R_KB + h)),
        pl.BlockSpec((ATTN_BLOCK, HEAD_DIM), lambda b, h, n: (prev_blk(b, n), P_VB + h)),
    ]
    return pl.pallas_call(
        _swa_body, grid=(bsz, B_KV_HEADS, nsb), in_specs=in_specs,
        out_specs=pl.BlockSpec((SWA_ROWS, gw), lambda b, h, n: (b * nsb + n, h)),
        out_shape=jax.ShapeDtypeStruct((bsz * tlen, B_HEADS * HEAD_DIM), F32),
        compiler_params=_cparams(("parallel", "parallel", "parallel"), 32), name="swa_attn",
    )(sink, r_arr, r_arr, p_arr, r_arr, p_arr)


def _dilated_body(*refs, tlen):
    qkv = refs[:3 * N_C_GROUPS]
    yc_ref = refs[3 * N_C_GROUPS]
    o_scr = refs[3 * N_C_GROUPS + 1:3 * N_C_GROUPS + 1 + N_C_GROUPS]
    l_scr = refs[3 * N_C_GROUPS + 1 + N_C_GROUPS:]
    blk = ATTN_BLOCK
    for g, (win, dil) in enumerate(C_PATTERNS):
        q_ref, k_ref, v_ref = qkv[3 * g:3 * g + 3]
        for res in range(dil):
            for i in range(tlen // dil // blk):
                def rows(ii):
                    start = res + dil * blk * ii
                    return pl.ds(start, blk) if dil == 1 else pl.ds(start, blk, stride=dil)
                cur = rows(i)
                if i == 0:
                    k_prev = v_prev = None
                else:
                    k_prev, v_prev = k_ref[rows(i - 1), :], v_ref[rows(i - 1), :]
                o, lse = _attn_block(q_ref[cur, :].astype(BF16), k_prev, k_ref[cur, :], v_prev, v_ref[cur, :], True, None)
                o_scr[g][cur, :] = o
                l_scr[g][cur, :] = jnp.broadcast_to(lse, (blk, HEAD_DIM))
    step = 2 * blk
    for ch in range(tlen // step):
        rows = slice(ch * step, (ch + 1) * step)
        ls = [l[rows, :] for l in l_scr]
        lmax = jnp.maximum(jnp.maximum(ls[0], ls[1]), ls[2])
        es = [jnp.exp(l - lmax) for l in ls]
        num = es[0] * o_scr[0][rows, :] + es[1] * o_scr[1][rows, :] + es[2] * o_scr[2][rows, :]
        yc_ref[rows, :] = num / (es[0] + es[1] + es[2])


def _dilated_attention(r_arr, p_arr, bsz, tlen):
    in_specs, args = [], []
    for g, (win, dil) in enumerate(C_PATTERNS):
        assert win // dil == ATTN_BLOCK and tlen % (dil * ATTN_BLOCK) == 0
        in_specs += [pl.BlockSpec((tlen, HEAD_DIM), lambda b, h, g=g: (b, R_QC + C_HEADS * g + h)),
                     pl.BlockSpec((tlen, HEAD_DIM), lambda b, h, g=g: (b, R_KC + C_HEADS * g + h)),
                     pl.BlockSpec((tlen, HEAD_DIM), lambda b, h, g=g: (b, P_VC + C_HEADS * g + h))]
        args += [r_arr, r_arr, p_arr]
    return pl.pallas_call(
        functools.partial(_dilated_body, tlen=tlen), grid=(bsz, C_HEADS), in_specs=in_specs,
        out_specs=pl.BlockSpec((tlen, HEAD_DIM), lambda b, h: (b, h)),
        out_shape=jax.ShapeDtypeStruct((bsz * tlen, C_HEADS * HEAD_DIM), F32),
        scratch_shapes=[pltpu.VMEM((tlen, HEAD_DIM), F32) for _ in range(2 * N_C_GROUPS)],
        compiler_params=_cparams(("parallel", "parallel"), 40), name="dilated_attn",
    )(*args)


def _sample_body(sink_ref, q_ref, kn_ref, vn_ref, qk_ref, va_ref, ga_ref, lr_ref, st_ref,
                 cb_ref, c1_ref, c2_ref, c3_ref, wlr_ref, blr_ref, ng_ref,
                 ya_ref, ob_ref, yc_ref, sn_ref):
    nt = (((1,), (1,)), ((), ()))

    z = jnp.dot(lr_ref[...], wlr_ref[...], precision=HIGHEST, preferred_element_type=F32)[0:1, :] + blr_ref[...]
    a_all = jnp.exp(_log_sigmoid(z) / A_GATE_NORM)
    qk = qk_ref[...]
    rid16 = lax.broadcasted_iota(jnp.int32, (16, 1), 0)
    rows = jnp.zeros((16, A_DK), F32)
    for h in range(A_HEADS):
        a_h = jnp.broadcast_to(a_all[:, h * A_DK:(h + 1) * A_DK], (16, A_DK))
        k_h = jnp.broadcast_to(qk[A_HEADS + h:A_HEADS + h + 1, :], (16, A_DK))
        q_h = jnp.broadcast_to(qk[h:h + 1, :], (16, A_DK)) * (A_DK ** -0.5)
        rows = jnp.where(rid16 == 3 * h, a_h, rows)
        rows = jnp.where(rid16 == 3 * h + 1, k_h, rows)
        rows = jnp.where(rid16 == 3 * h + 2, q_h, rows)
    cols = jnp.concatenate([rows, jnp.zeros((LANES - 16, A_DK), F32)], axis=0).T
    for h in range(A_HEADS):
        a_col, k_col, q_col = cols[:, 3 * h:3 * h + 1], cols[:, 3 * h + 1:3 * h + 2], cols[:, 3 * h + 2:3 * h + 3]
        s_new = a_col * st_ref[h] + k_col * va_ref[h:h + 1, :]
        sn_ref[h] = s_new
        o = jnp.sum(q_col * s_new, axis=0, keepdims=True)
        ya_ref[h:h + 1, :] = _rms_gate(o, ng_ref[...], ga_ref[h:h + 1, :])

    q = q_ref[...]
    q16 = q.astype(BF16)
    rid = lax.broadcasted_iota(jnp.int32, (S_ROWS, 1), 0)

    def pick_b(j):
        return (rid < B_HEADS) & (rid // B_GROUP == j)

    def pick_c(g, hh):
        return rid == 16 + SUBLANES * g + hh

    c_refs = (c1_ref, c2_ref, c3_ref)
    s = jnp.zeros((S_ROWS, ATTN_BLOCK), F32)
    for j in range(B_KV_HEADS):
        kj = cb_ref[0, :, j, :].astype(BF16)
        s = jnp.where(pick_b(j), lax.dot_general(q16, kj, nt, preferred_element_type=F32), s)
    for g in range(N_C_GROUPS):
        for hh in range(C_HEADS):
            kh = c_refs[g][0, :, hh, :].astype(BF16)
            s = jnp.where(pick_c(g, hh), lax.dot_general(q16, kh, nt, preferred_element_type=F32), s)
    s = s * ATTN_SCALE
    s_n = jnp.sum(q * kn_ref[...], axis=1, keepdims=True) * ATTN_SCALE
    sink = jnp.full((S_ROWS, 1), NEG_INF, F32)
    for i in range(B_HEADS):
        sink = jnp.where(rid == i, sink_ref[i], sink)
    m = jnp.maximum(jnp.maximum(jnp.max(s, axis=1, keepdims=True), s_n), sink)
    p = jnp.exp(s - m)
    p_n = jnp.exp(s_n - m)
    den = jnp.sum(p, axis=1, keepdims=True) + p_n + jnp.exp(sink - m)
    p16 = p.astype(BF16)
    acc = p_n * vn_ref[...]
    for j in range(B_KV_HEADS):
        vj = cb_ref[1, :, j, :].astype(BF16)
        acc = acc + jnp.where(pick_b(j), jnp.dot(p16, vj, preferred_element_type=F32), 0.0)
    for g in range(N_C_GROUPS):
        for hh in range(C_HEADS):
            vh = c_refs[g][1, :, hh, :].astype(BF16)
            acc = acc + jnp.where(pick_c(g, hh), jnp.dot(p16, vh, preferred_element_type=F32), 0.0)
    o = acc / den
    lse = m + jnp.log(den)
    ob_ref[...] = o[0:16, :]
    ls = [lse[16 + SUBLANES * g:24 + SUBLANES * g, :] for g in range(N_C_GROUPS)]
    os_ = [o[16 + SUBLANES * g:24 + SUBLANES * g, :] for g in range(N_C_GROUPS)]
    lmax = jnp.maximum(jnp.maximum(ls[0], ls[1]), ls[2])
    es = [jnp.exp(l - lmax) for l in ls]
    yc_ref[...] = (es[0] * os_[0] + es[1] * os_[1] + es[2] * os_[2]) / (es[0] + es[1] + es[2])


def _sample_mixers(layer, r_s, p_s, state_gla, cache_b, caches_c, sink, wlr_pad, blr, ng):
    bs = r_s.shape[0]
    r3 = r_s.reshape(bs, U_PROJ, LANES)
    p3 = p_s.reshape(bs, U_PROJ, LANES)
    z4 = jnp.zeros((bs, 4, LANES), F32)

    def qrows(x3, ub, uc, rep):
        parts = [jnp.repeat(x3[:, ub:ub + B_KV_HEADS], B_GROUP, axis=1) if rep else x3[:, ub:ub + B_HEADS], z4]
        for g in range(N_C_GROUPS):
            parts += [x3[:, uc + C_HEADS * g:uc + C_HEADS * (g + 1)], z4]
        return jnp.concatenate(parts, axis=1)

    q_rows = qrows(r3, R_QB, R_QC, False)
    kn_rows = qrows(r3, R_KB, R_KC, True)
    vn_rows = qrows(p3, P_VB, P_VC, True)
    qk_a = p3[:, P_QA:P_VA]
    v_a = p_s[:, P_VA * LANES:P_GA * LANES].reshape(bs, A_HEADS, A_DV)
    g_a = p_s[:, P_GA * LANES:P_VB * LANES].reshape(bs, A_HEADS, A_DV)
    lr = jnp.broadcast_to(p3[:, P_LR:P_LR + 1], (bs, SUBLANES, LANES))

    cc = [c.reshape(bs, DEPTH, 2, ATTN_BLOCK, dil, C_HEADS, HEAD_DIM) for c, (_, dil) in zip(caches_c, C_PATTERNS)]

    def tok(rows, width):
        return pl.BlockSpec((None, rows, width), lambda b: (b, 0, 0))

    in_specs = [
        pl.BlockSpec(memory_space=pltpu.SMEM),
        tok(S_ROWS, LANES), tok(S_ROWS, LANES), tok(S_ROWS, LANES),
        tok(2 * A_HEADS, LANES), tok(A_HEADS, A_DV), tok(A_HEADS, A_DV), tok(SUBLANES, LANES),
        pl.BlockSpec((None, None, A_HEADS, A_DK, A_DV), lambda b: (b, layer, 0, 0, 0)),
        pl.BlockSpec((None, None, 2, B_WINDOW, B_KV_HEADS, HEAD_DIM), lambda b: (b, layer, 0, 0, 0, 0)),
    ] + [pl.BlockSpec((None, None, 2, ATTN_BLOCK, None, C_HEADS, HEAD_DIM), lambda b: (b, layer, 0, 0, 0, 0, 0)) for _ in cc] + [
        pl.BlockSpec((LANES, A_HEADS * A_DK), lambda b: (0, 0)),
        pl.BlockSpec((1, A_HEADS * A_DK), lambda b: (0, 0)),
        pl.BlockSpec((1, A_DV), lambda b: (0, 0)),
    ]
    out_specs = [
        pl.BlockSpec((None, A_HEADS, A_DV), lambda b: (b, 0, 0)),
        pl.BlockSpec((None, 16, HEAD_DIM), lambda b: (b, 0, 0)),
        pl.BlockSpec((None, SUBLANES, HEAD_DIM), lambda b: (b, 0, 0)),
        pl.BlockSpec((None, A_HEADS, A_DK, A_DV), lambda b: (b, 0, 0, 0)),
    ]
    out_shape = [
        jax.ShapeDtypeStruct((bs, A_HEADS, A_DV), F32),
        jax.ShapeDtypeStruct((bs, 16, HEAD_DIM), F32),
        jax.ShapeDtypeStruct((bs, SUBLANES, HEAD_DIM), F32),
        jax.ShapeDtypeStruct((bs, A_HEADS, A_DK, A_DV), F32),
    ]
    ya, ob, yc, sn = pl.pallas_call(
        _sample_body, grid=(bs,), in_specs=in_specs, out_specs=out_specs, out_shape=out_shape,
        compiler_params=_cparams(("parallel",), 32), name="sample_mixers",
    )(sink, q_rows, kn_rows, vn_rows, qk_a, v_a, g_a, lr, state_gla, cache_b, *cc, wlr_pad, blr, ng)
    return (ya.reshape(bs, A_HEADS * A_DV), ob[:, :B_HEADS].reshape(bs, B_HEADS * HEAD_DIM),
            yc[:, :C_HEADS].reshape(bs, C_HEADS * HEAD_DIM), sn)


def _outproj_body(ya_ref, ob_ref, yc_ref, wa_ref, wb_ref, wc_ref, bias_ref, h_ref, y_ref):
    acc = jnp.dot(ya_ref[...].astype(BF16), wa_ref[...], preferred_element_type=F32)
    acc = acc + jnp.dot(ob_ref[...].astype(BF16), wb_ref[...], preferred_element_type=F32)
    acc = acc + jnp.dot(yc_ref[...].astype(BF16), wc_ref[...], preferred_element_type=F32)
    y_ref[...] = DN_ALPHA * h_ref[...] + acc + bias_ref[...]


def _outproj(ya, ob, yc, wa, wb, wc, bias, h, *, tm, tn=1024):
    m = h.shape[0]

    def rowspec(x):
        return pl.BlockSpec((tm, x.shape[1]), lambda j, i: (i, 0))

    def wspec(w):
        return pl.BlockSpec((w.shape[0], tn), lambda j, i: (0, j))

    return pl.pallas_call(
        _outproj_body, grid=(D_MODEL // tn, m // tm),
        in_specs=[rowspec(ya), rowspec(ob), rowspec(yc), wspec(wa), wspec(wb), wspec(wc),
                  pl.BlockSpec((1, tn), lambda j, i: (0, j)), pl.BlockSpec((tm, tn), lambda j, i: (i, j))],
        out_specs=pl.BlockSpec((tm, tn), lambda j, i: (i, j)),
        out_shape=jax.ShapeDtypeStruct((m, D_MODEL), F32),
        compiler_params=_cparams(("parallel", "parallel"), 48), name="outproj",
    )(ya, ob, yc, wa, wb, wc, bias, h)


def _pack_bf16_pairs(x):
    half = x.shape[1] // 2
    bits = lax.bitcast_convert_type(x.astype(BF16).astype(F32), jnp.uint32)
    return (bits[:, :half] >> 16) | bits[:, half:]


def _unpack_bf16_pairs(w):
    lo = lax.bitcast_convert_type(w << 16, F32).astype(BF16)
    hi = lax.bitcast_convert_type(w & jnp.uint32(0xFFFF0000), F32).astype(BF16)
    return lo, hi


def _ln_router_body(y_ref, g_ref, b_ref, rw_ref, rb_ref, h_ref, pk_ref, lg_ref, *, n_real):
    i = pl.program_id(0)

    @pl.when(i < n_real)
    def _():
        h = _ln_rows(y_ref[...], g_ref[...], b_ref[...])
        h_ref[...] = h
        pk_ref[...] = _pack_bf16_pairs(h)
        lg_ref[...] = jnp.dot(h, rw_ref[...], precision=HIGHEST, preferred_element_type=F32) + rb_ref[...]

    @pl.when(i >= n_real)
    def _():
        h_ref[...] = jnp.zeros_like(h_ref)
        pk_ref[...] = jnp.zeros_like(pk_ref)
        lg_ref[...] = jnp.zeros_like(lg_ref)


def _ln_router(y, g, b, rw_pad, rb_pad, tm, extra_blocks=0):
    m, d = y.shape
    n_real = m // tm
    n_out = n_real + extra_blocks
    row_in = pl.BlockSpec((tm, d), lambda i: (jnp.minimum(i, n_real - 1), 0))
    vec = pl.BlockSpec((1, d), lambda i: (0, 0))
    return pl.pallas_call(
        functools.partial(_ln_router_body, n_real=n_real), grid=(n_out,),
        in_specs=[row_in, vec, vec, pl.BlockSpec((d, LANES), lambda i: (0, 0)), pl.BlockSpec((1, LANES), lambda i: (0, 0))],
        out_specs=[pl.BlockSpec((tm, d), lambda i: (i, 0)), pl.BlockSpec((tm, d // 2), lambda i: (i, 0)),
                   pl.BlockSpec((tm, LANES), lambda i: (i, 0))],
        out_shape=[jax.ShapeDtypeStruct((n_out * tm, d), F32), jax.ShapeDtypeStruct((n_out * tm, d // 2), jnp.uint32),
                   jax.ShapeDtypeStruct((n_out * tm, LANES), F32)],
        compiler_params=_cparams(("arbitrary",), 48), name="ln_router",
    )(y, g, b, rw_pad, rb_pad)


def _moe_gather_body(src_ref, na_ref, h_hbm, xs_ref, buf, sem, *, tm):
    i = pl.program_id(0)
    na = na_ref[0]

    def row_copy(blk, slot, r):
        return pltpu.make_async_copy(h_hbm.at[pl.ds(src_ref[blk * tm + r], 1), :], buf.at[slot, pl.ds(r, 1), :],
                                     sem.at[slot])

    def issue(blk, slot):
        def body(r2, carry):
            row_copy(blk, slot, 2 * r2).start(priority=0)
            row_copy(blk, slot, 2 * r2 + 1).start(priority=1)
            return carry
        lax.fori_loop(0, tm // 2, body, 0, unroll=4)

    @pl.when(i == 0)
    def _():
        issue(0, 0)

    slot = i % 2

    @pl.when(i + 1 < na)
    def _():
        issue(i + 1, 1 - slot)

    @pl.when(i < na)
    def _():
        def wait_body(r, carry):
            row_copy(i, slot, r).wait()
            return carry
        lax.fori_loop(0, tm, wait_body, 0, unroll=8)
        half = buf.shape[2]
        lo, hi = _unpack_bf16_pairs(buf[slot])
        xs_ref[:, :half] = lo
        xs_ref[:, half:] = hi

    @pl.when(i >= na)
    def _():
        xs_ref[...] = jnp.zeros_like(xs_ref)


def _moe_gather(h_packed, src, n_active, tm):
    half = h_packed.shape[1]
    n_blk = src.shape[0] // tm
    assert tm % 2 == 0
    return pl.pallas_call(
        functools.partial(_moe_gather_body, tm=tm),
        grid_spec=pltpu.PrefetchScalarGridSpec(
            num_scalar_prefetch=2, grid=(n_blk,), in_specs=[pl.BlockSpec(memory_space=pl.ANY)],
            out_specs=pl.BlockSpec((tm, 2 * half), lambda i, sr, na: (i, 0)),
            scratch_shapes=[pltpu.VMEM((2, tm, half), jnp.uint32), pltpu.SemaphoreType.DMA((2,))]),
        out_shape=jax.ShapeDtypeStruct((n_blk * tm, 2 * half), BF16),
        compiler_params=_cparams(("arbitrary",), 40), name="moe_gather",
    )(src, n_active, h_packed)


def _weight_stream(w_hbms, stages, sem, be_ref, nb_ref, na_ref, cnt_ref, *, layer, tn, on_ready):
    j = pl.program_id(0)
    i = pl.program_id(1)
    n_j = pl.num_programs(0)
    na = na_ref[0]
    active = i < na
    fresh = active & ((i == 0) | (be_ref[i] != be_ref[jnp.maximum(i - 1, 0)]))

    def copies(e, jj, slot):
        cols = pl.ds(pl.multiple_of(jj * tn, tn), tn)
        return [pltpu.make_async_copy(w.at[layer, e, :, cols], st.at[slot], sem.at[t, slot])
                for t, (w, st) in enumerate(zip(w_hbms, stages))]

    @pl.when((j == 0) & (i == 0))
    def _():
        cnt_ref[0] = 0
        for cp in copies(be_ref[0], 0, 0):
            cp.start()

    @pl.when(fresh)
    def _():
        slot = cnt_ref[0] % 2
        for cp in copies(be_ref[i], j, slot):
            cp.wait()
        nxt = nb_ref[i]
        same_j = nxt < na
        e_next = be_ref[jnp.where(same_j, nxt, 0)]
        j_next = jnp.where(same_j, j, j + 1)

        @pl.when(same_j | (j + 1 < n_j))
        def _():
            for cp in copies(e_next, j_next, 1 - slot):
                cp.start()

        on_ready(slot)
        cnt_ref[0] = cnt_ref[0] + 1

    return active


def _moe_up_body(be_ref, nb_ref, na_ref, x_ref, bg_ref, bu_ref, wg_hbm, wu_hbm, a_ref,
                 stage_g, stage_u, wg_bf, wu_bf, sem, cnt_ref, *, layer, tn):
    def on_ready(slot):
        wg_bf[...] = stage_g[slot].astype(BF16)
        wu_bf[...] = stage_u[slot].astype(BF16)

    active = _weight_stream((wg_hbm, wu_hbm), (stage_g, stage_u), sem, be_ref, nb_ref, na_ref, cnt_ref,
                            layer=layer, tn=tn, on_ready=on_ready)

    @pl.when(active)
    def _():
        x = x_ref[...]
        g = jnp.dot(x, wg_bf[...], preferred_element_type=F32) + bg_ref[...]
        u = jnp.dot(x, wu_bf[...], preferred_element_type=F32) + bu_ref[...]
        g = jnp.minimum(g, SWIGLU_LIMIT)
        u = jnp.clip(u, -SWIGLU_LIMIT, SWIGLU_LIMIT)
        a_ref[...] = (g * jax.nn.sigmoid(SWIGLU_ALPHA * g) * (u + 1.0)).astype(BF16)

    @pl.when(jnp.logical_not(active))
    def _():
        a_ref[...] = jnp.zeros_like(a_ref)


def _moe_down_body(be_ref, nb_ref, na_ref, a_ref, bd_ref, wd_hbm, y_ref, stage_d, wd_bf, sem, cnt_ref, *, layer, tn):
    def on_ready(slot):
        wd_bf[...] = stage_d[slot].astype(BF16)

    active = _weight_stream((wd_hbm,), (stage_d,), sem, be_ref, nb_ref, na_ref, cnt_ref,
                            layer=layer, tn=tn, on_ready=on_ready)

    @pl.when(active)
    def _():
        y_ref[...] = jnp.dot(a_ref[...], wd_bf[...], preferred_element_type=F32) + bd_ref[...]

    @pl.when(jnp.logical_not(active))
    def _():
        y_ref[...] = jnp.zeros_like(y_ref)


def _moe_block_rows(n_rows):
    mean = n_rows / N_EXPERTS
    return int(-(-(mean + 2.0 * mean ** 0.5) // (3 * 2 * SUBLANES)) * 2 * SUBLANES)


def _moe_experts(layer, xs, block_expert, next_block, n_active, w_gate, b_gate, w_up, b_up, w_down, b_down, tm):
    m_pad, d = xs.shape
    d_ff = w_gate.shape[-1]
    n_blk = m_pad // tm

    def blk(i, na):
        return jnp.minimum(i, na[0] - 1)

    def rows(width):
        return pl.BlockSpec((tm, width), lambda j, i, be, nb, na: (blk(i, na), 0))

    def bias(t):
        return pl.BlockSpec((None, None, 1, t), lambda j, i, be, nb, na: (layer, be[i], 0, j))

    def out(t):
        return pl.BlockSpec((tm, t), lambda j, i, be, nb, na: (i, j))

    hbm = pl.BlockSpec(memory_space=pl.ANY)
    smem_cnt = pltpu.SMEM((1,), jnp.int32)

    tn = MOE_TN_UP
    act = pl.pallas_call(
        functools.partial(_moe_up_body, layer=layer, tn=tn),
        grid_spec=pltpu.PrefetchScalarGridSpec(
            num_scalar_prefetch=3, grid=(d_ff // tn, n_blk),
            in_specs=[rows(d), bias(tn), bias(tn), hbm, hbm],
            out_specs=out(tn),
            scratch_shapes=[pltpu.VMEM((2, d, tn), F32), pltpu.VMEM((2, d, tn), F32),
                            pltpu.VMEM((d, tn), BF16), pltpu.VMEM((d, tn), BF16),
                            pltpu.SemaphoreType.DMA((2, 2)), smem_cnt]),
        out_shape=jax.ShapeDtypeStruct((m_pad, d_ff), BF16),
        compiler_params=_cparams(("arbitrary", "arbitrary"), 56), name="moe_gate_up",
    )(block_expert, next_block, n_active, xs,
      b_gate.reshape(DEPTH, N_EXPERTS, 1, d_ff), b_up.reshape(DEPTH, N_EXPERTS, 1, d_ff), w_gate, w_up)

    tn2 = MOE_TN_DOWN
    return pl.pallas_call(
        functools.partial(_moe_down_body, layer=layer, tn=tn2),
        grid_spec=pltpu.PrefetchScalarGridSpec(
            num_scalar_prefetch=3, grid=(d // tn2, n_blk),
            in_specs=[rows(d_ff), bias(tn2), hbm],
            out_specs=out(tn2),
            scratch_shapes=[pltpu.VMEM((2, d_ff, tn2), F32), pltpu.VMEM((d_ff, tn2), BF16),
                            pltpu.SemaphoreType.DMA((1, 2)), smem_cnt]),
        out_shape=jax.ShapeDtypeStruct((m_pad, d), F32),
        compiler_params=_cparams(("arbitrary", "arbitrary"), 56), name="moe_down",
    )(block_expert, next_block, n_active, act, b_down.reshape(DEPTH, N_EXPERTS, 1, d), w_down)


def _combine_body(dest_ref, h_ref, gt_ref, g_ref, b_ref, yb_hbm, o_ref, ob_ref, yg, sem, *, tm):
    t = pl.program_id(0)
    n_tiles = pl.num_programs(0)

    def row_copy(tile, slot, r, kk):
        d = dest_ref[(tile * tm + r) * TOP_K + kk]
        return pltpu.make_async_copy(yb_hbm.at[pl.ds(d, 1), :], yg.at[slot, kk, pl.ds(r, 1), :], sem.at[slot])

    def issue(tile, slot):
        def body(r, carry):
            for kk in range(TOP_K):
                row_copy(tile, slot, r, kk).start(priority=kk % 2)
            return carry
        lax.fori_loop(0, tm, body, 0, unroll=4)

    @pl.when(t == 0)
    def _():
        issue(0, 0)

    slot = t % 2

    @pl.when(t + 1 < n_tiles)
    def _():
        issue(t + 1, 1 - slot)

    def wait_body(r, carry):
        for kk in range(TOP_K):
            row_copy(t, slot, r, kk).wait()
        return carry
    lax.fori_loop(0, tm, wait_body, 0, unroll=4)

    gt = gt_ref[...]
    moe = ((yg[slot, 0] * gt[:, 0:1] + yg[slot, 1] * gt[:, 1:2])
           + (yg[slot, 2] * gt[:, 2:3] + yg[slot, 3] * gt[:, 3:4]))
    y = _ln_rows(DN_ALPHA * h_ref[...] + moe, g_ref[...], b_ref[...])
    o_ref[...] = y
    ob_ref[...] = y.astype(BF16)


def _combine_ln(h1, yb, dest, gates, g, b, tm, m):
    d = h1.shape[1]
    row = pl.BlockSpec((tm, d), lambda i, dr: (i, 0))
    vec = pl.BlockSpec((1, d), lambda i, dr: (0, 0))
    return pl.pallas_call(
        functools.partial(_combine_body, tm=tm),
        grid_spec=pltpu.PrefetchScalarGridSpec(
            num_scalar_prefetch=1, grid=(m // tm,),
            in_specs=[row, pl.BlockSpec((tm, TOP_K), lambda i, dr: (i, 0)), vec, vec,
                      pl.BlockSpec(memory_space=pl.ANY)],
            out_specs=[row, row],
            scratch_shapes=[pltpu.VMEM((2, TOP_K, tm, d), F32), pltpu.SemaphoreType.DMA((2,))]),
        out_shape=[jax.ShapeDtypeStruct((m, d), F32), jax.ShapeDtypeStruct((m, d), BF16)],
        compiler_params=_cparams(("arbitrary",), 48), name="combine_ln2",
    )(dest, h1, gates, g, b, yb)


def _route(logits, tm):
    n_tok = logits.shape[0]
    top_v, top_e = lax.top_k(logits, TOP_K)
    gates = jax.nn.softmax(top_v, axis=-1)
    m = n_tok * TOP_K
    e_flat = top_e.reshape(m)
    n_blk = -(-m // tm) + N_EXPERTS
    onehot = jax.nn.one_hot(e_flat, N_EXPERTS, dtype=jnp.int32)
    counts = jnp.sum(onehot, axis=0)
    first = jnp.cumsum(counts) - counts
    order = jnp.argsort(e_flat).astype(jnp.int32)
    pos = jnp.argsort(order).astype(jnp.int32)
    blocks_per = (counts + tm - 1) // tm
    blocks_end = jnp.cumsum(blocks_per)
    row0 = (blocks_end - blocks_per) * tm
    dest = pos + jnp.sum(onehot * (row0 - first)[None, :], axis=1)
    n_active = blocks_end[-1]
    bidx = jnp.minimum(jnp.arange(n_blk), n_active - 1)
    block_expert = jnp.minimum(jnp.sum(bidx[:, None] >= blocks_end[None, :], axis=1), N_EXPERTS - 1).astype(jnp.int32)
    next_block = blocks_end[block_expert].astype(jnp.int32)
    off = (jnp.arange(n_blk)[:, None] * tm - row0[block_expert][:, None]) + jnp.arange(tm)[None, :]
    valid = (off < counts[block_expert][:, None]) & (jnp.arange(n_blk)[:, None] < n_active)
    sorted_pos = jnp.clip(first[block_expert][:, None] + off, 0, m - 1)
    src = jnp.where(valid, order[sorted_pos] // TOP_K, 0).reshape(n_blk * tm)
    return gates, dest.astype(jnp.int32), src, block_expert, next_block, n_active.reshape(1).astype(jnp.int32)


def _rope_tables(pos):
    half = HEAD_DIM // 2
    inv_freq = ROPE_THETA ** (-np.arange(half, dtype=np.float64) / half)
    ang = np.asarray(pos, np.float64)[:, None] * inv_freq[None, :]
    cos, sin = np.cos(ang), np.sin(ang)
    return (jnp.asarray(np.concatenate([cos, cos], axis=1), F32),
            jnp.asarray(np.concatenate([-sin, sin], axis=1), F32))


def _split_w_in(w, b):
    k = w.shape[0]
    w = w.astype(BF16)
    w_r = jnp.concatenate([w[:, _O_QB:_O_VB], w[:, _O_QC:_O_VC], jnp.zeros((k, LANES), w.dtype)], axis=1)
    w_p = jnp.concatenate([w[:, _O_QA:_O_LR], w[:, _O_VB:_O_QC], w[:, _O_VC:_O_END], w[:, _O_LR:_O_QB],
                           jnp.zeros((k, LANES - A_LOWRANK), w.dtype)], axis=1)
    b_r = jnp.concatenate([b[_O_QB:_O_VB], b[_O_QC:_O_VC], jnp.zeros((LANES,), b.dtype)])
    b_p = jnp.concatenate([b[_O_QA:_O_LR], b[_O_VB:_O_QC], b[_O_VC:_O_END], b[_O_LR:_O_QB],
                           jnp.zeros((LANES - A_LOWRANK,), b.dtype)])
    return w_r, w_p, b_r.reshape(1, W_PROJ), b_p.reshape(1, W_PROJ)


def _mix_prompt(rp, pp, wlr_pad, blr, ng, sink, bp, tlen):
    ya, st = _gla_prompt(pp, wlr_pad, blr, ng, bp, tlen)
    ob = _swa_attention(rp, pp, sink, bp, tlen)
    yc = _dilated_attention(rp, pp, bp, tlen)
    return ya, ob, yc, st


def kernel(x_prompt, x_sample, cache_b_kv, cache_c1_kv, cache_c2_kv, cache_c3_kv, state_gla, ln_in_g, ln_in_b, w_in, b_in, gla_w_gate, gla_b_gate, gla_norm_g, attn_sinks, w_out, b_out, ln1_g, ln1_b, router_w, router_b, w_gate, b_gate, w_up, b_up, w_down, b_down, ln2_g, ln2_b):
    bp, tlen, d = x_prompt.shape
    bs, slen, _ = x_sample.shape
    assert slen == 1 and d == D_MODEL and tlen % SWA_ROWS == 0
    n_p, n_s = bp * tlen, bs * slen
    tm_p = 256

    cos_p, sin_p = _rope_tables(np.tile(np.arange(tlen), bp))
    cos_s, sin_s = _rope_tables(np.full((n_s,), PAST_LEN))
    caches_c = (cache_c1_kv, cache_c2_kv, cache_c3_kv)

    hp, hp_b = _ln_in(x_prompt.reshape(n_p, d), ln_in_g, ln_in_b, tm_p)
    hs, hs_b = _ln_in(x_sample.reshape(n_s, d), ln_in_g, ln_in_b, n_s)

    new_p = [[] for _ in range(5)]
    new_s = [[] for _ in range(5)]
    for l in range(DEPTH):
        w_r, w_p, b_r, b_p = _split_w_in(w_in[l], b_in[l])
        wlr_pad = jnp.concatenate([gla_w_gate[l], jnp.zeros((LANES - A_LOWRANK, A_HEADS * A_DK), F32)], axis=0)
        blr = gla_b_gate[l].reshape(1, A_HEADS * A_DK)
        ng = gla_norm_g[l].reshape(1, A_DV)
        sink = attn_sinks[l].reshape(B_HEADS)
        wo = w_out[l].astype(BF16)
        na, nb = A_HEADS * A_DV, B_HEADS * HEAD_DIM
        wo_a, wo_b, wo_c = wo[:na], wo[na:na + nb], wo[na + nb:]
        bo = b_out[l].reshape(1, d)

        rp = _proj(hp_b, w_r, b_r, cos_p, sin_p, tm=512)
        pp = _proj(hp_b, w_p, b_p, tm=512)
        rs = _proj(hs_b, w_r, b_r, cos_s, sin_s, tm=n_s, tn=W_PROJ // 2)
        ps = _proj(hs_b, w_p, b_p, tm=n_s, tn=W_PROJ // 2)

        ya_p, ob_p, yc_p, st_p = _mix_prompt(rp, pp, wlr_pad, blr, ng, sink, bp, tlen)
        ya_s, ob_s, yc_s, st_s = _sample_mixers(l, rs, ps, state_gla, cache_b_kv, caches_c, sink, wlr_pad, blr, ng)

        rp4 = rp.reshape(bp, tlen, W_PROJ)
        pp4 = pp.reshape(bp, tlen, W_PROJ)

        def kv_prompt(uk, uv, heads, rows):
            kk = rp4[:, tlen - rows:, uk * LANES:(uk + heads) * LANES].reshape(bp, rows, heads, HEAD_DIM)
            vv = pp4[:, tlen - rows:, uv * LANES:(uv + heads) * LANES].reshape(bp, rows, heads, HEAD_DIM)
            return jnp.stack([kk, vv], axis=1)

        def kv_sample(uk, uv, heads):
            kk = rs[:, uk * LANES:(uk + heads) * LANES].reshape(bs, slen, heads, HEAD_DIM)
            vv = ps[:, uv * LANES:(uv + heads) * LANES].reshape(bs, slen, heads, HEAD_DIM)
            return jnp.stack([kk, vv], axis=1)

        new_p[0].append(kv_prompt(R_KB, P_VB, B_KV_HEADS, min(B_WINDOW, tlen)))
        new_s[0].append(kv_sample(R_KB, P_VB, B_KV_HEADS))
        for g, (win, dil) in enumerate(C_PATTERNS):
            new_p[1 + g].append(kv_prompt(R_KC + C_HEADS * g, P_VC + C_HEADS * g, C_HEADS, min(win, tlen)))
            new_s[1 + g].append(kv_sample(R_KC + C_HEADS * g, P_VC + C_HEADS * g, C_HEADS))
        new_p[4].append(st_p)
        new_s[4].append(st_s)

        y_p = _outproj(ya_p, ob_p, yc_p, wo_a, wo_b, wo_c, bo, hp, tm=512)
        y_s = _outproj(ya_s, ob_s, yc_s, wo_a, wo_b, wo_c, bo, hs, tm=n_s, tn=D_MODEL // 2)
        rw_pad = jnp.concatenate([router_w[l], jnp.zeros((d, LANES - N_EXPERTS), F32)], axis=1)
        rb_pad = jnp.concatenate([router_b[l], jnp.zeros((LANES - N_EXPERTS,), F32)]).reshape(1, LANES)
        g1, b1 = ln1_g[l].reshape(1, d), ln1_b[l].reshape(1, d)
        h1_p, pk_p, lg_p = _ln_router(y_p, g1, b1, rw_pad, rb_pad, tm_p, extra_blocks=1)
        h1_s, pk_s, lg_s = _ln_router(y_s, g1, b1, rw_pad, rb_pad, n_s)
        pk_all = lax.dynamic_update_slice(pk_p, pk_s, (n_p, 0))

        logits = jnp.concatenate([lg_p[:n_p], lg_s], axis=0)[:, :N_EXPERTS]
        moe_tm = _moe_block_rows((n_p + n_s) * TOP_K)
        gates, dest, src, block_expert, next_block, n_active = _route(logits, moe_tm)
        xs = _moe_gather(pk_all, src, n_active, moe_tm)
        yb = _moe_experts(l, xs, block_expert, next_block, n_active, w_gate, b_gate, w_up, b_up, w_down, b_down, moe_tm)
        g2, b2 = ln2_g[l].reshape(1, d), ln2_b[l].reshape(1, d)
        hp, hp_b = _combine_ln(h1_p, yb, dest[:n_p * TOP_K], gates[:n_p], g2, b2, 128, n_p)
        hs, hs_b = _combine_ln(h1_s, yb, dest[n_p * TOP_K:], gates[n_p:], g2, b2, n_s, n_s)

    outs = [hp.reshape(bp, tlen, d), hs.reshape(bs, slen, d)]
    for i in range(5):
        outs += [jnp.stack(new_p[i], axis=1), jnp.stack(new_s[i], axis=1)]
    return tuple(outs)
```
